```python
import math
import jax, jax.numpy as jnp
from jax import lax
import numpy as np

D_MODEL = 1024
BATCH = 4
SEQ = 4096
DEPTH = 2

MEM_LEN = 256
EPS = 1e-6
GDN_HEADS = 4
GDN_DK = 128
GDN_DV = 128
GDN_CONV = 4
GDN_CHUNK = 64
QK_A = GDN_HEADS * GDN_DK
V_A = GDN_HEADS * GDN_DV
QKV_A = 2 * QK_A + V_A
CONV_CH = 512
CONV_K = 31
XA_HEADS = 4
XA_DH = 128
XA_W = XA_HEADS * XA_DH
N_BRANCH = 3
FFN_DIM = 2816
FFN_CONV = 3
IN_DIM = QKV_A + GDN_HEADS + GDN_HEADS + V_A + 2 * CONV_CH + XA_W + N_BRANCH * D_MODEL

kernel_name = "hybrid_gdn_conformer_xattn_block"


def _split_points():
    p0 = QKV_A
    p1 = p0 + GDN_HEADS
    p2 = p1 + GDN_HEADS
    p3 = p2 + V_A
    p4 = p3 + 2 * CONV_CH
    p5 = p4 + XA_W
    return [p0, p1, p2, p3, p4, p5]


def rmsnorm(x, w):
    xf = x.astype(jnp.float32)
    y = xf * lax.rsqrt(jnp.mean(xf * xf, axis=-1, keepdims=True) + EPS)
    return (y * w.astype(jnp.float32)).astype(x.dtype)


def layernorm(x, w, b):
    xf = x.astype(jnp.float32)
    mu = jnp.mean(xf, axis=-1, keepdims=True)
    var = jnp.mean(jnp.square(xf - mu), axis=-1, keepdims=True)
    y = (xf - mu) * lax.rsqrt(var + EPS)
    return (y * w.astype(jnp.float32) + b.astype(jnp.float32)).astype(x.dtype)


def causal_dwconv(x, w):
    K, C = w.shape
    return lax.conv_general_dilated(
        x, w[:, None, :].astype(x.dtype), window_strides=(1,), padding=[(K - 1, 0)],
        dimension_numbers=('NWC', 'WIO', 'NWC'), feature_group_count=C)


def l2norm(x):
    xf = x.astype(jnp.float32)
    return xf * lax.rsqrt(jnp.sum(xf * xf, axis=-1, keepdims=True) + EPS)


def gated_delta_rule(q, k, v, g, beta):
    Bn, Sn, H, dk = q.shape
    dv = v.shape[-1]
    C = GDN_CHUNK
    N = Sn // C
    f32 = jnp.float32

    def chunk(t):
        t = t.astype(f32).reshape((Bn, N, C, H) + t.shape[3:])
        return jnp.swapaxes(t, 2, 3)

    q = chunk(q) * (dk ** -0.5)
    k = chunk(k)
    v = chunk(v)
    beta = chunk(beta)
    g = jnp.cumsum(chunk(g), axis=-1)
    idx = jnp.arange(C)
    causal = idx[:, None] >= idx[None, :]
    strict = idx[:, None] > idx[None, :]
    decay = jnp.exp(jnp.where(causal, g[..., :, None] - g[..., None, :], -jnp.inf))
    kb = k * beta[..., None]
    A = jnp.where(strict, jnp.einsum('bnhid,bnhjd->bnhij', kb, k) * decay, 0.0)
    eye = jnp.eye(C, dtype=f32)
    T = lax.linalg.triangular_solve(eye + A, jnp.broadcast_to(eye, A.shape), left_side=True, lower=True)
    u = jnp.einsum('bnhij,bnhje->bnhie', T, v * beta[..., None])
    w = jnp.einsum('bnhij,bnhjd->bnhid', T, kb * jnp.exp(g)[..., None])
    qk = jnp.where(causal, jnp.einsum('bnhid,bnhjd->bnhij', q, k) * decay, 0.0)
    q_dec = q * jnp.exp(g)[..., None]
    g_last = g[..., -1]
    k_dec = k * jnp.exp(g_last[..., None] - g)[..., None]

    def step(state, inp):
        qd_c, kd_c, u_c, w_c, qk_c, gl_c = inp
        v_new = u_c - jnp.einsum('bhcd,bhde->bhce', w_c, state)
        o = jnp.einsum('bhcd,bhde->bhce', qd_c, state) + jnp.einsum('bhij,bhje->bhie', qk_c, v_new)
        state = state * jnp.exp(gl_c)[..., None, None] + jnp.einsum('bhcd,bhce->bhde', kd_c, v_new)
        return state, o

    xs = tuple(jnp.moveaxis(t, 1, 0) for t in (q_dec, k_dec, u, w, qk, g_last))
    s0 = jnp.zeros((Bn, H, dk, dv), f32)
    _, o = lax.scan(step, s0, xs)
    return jnp.transpose(o, (1, 0, 3, 2, 4)).reshape(Bn, Sn, H, dv)


def setup_inputs(seed: int = 0) -> dict:
    key = jax.random.key(seed)
    ks = jax.random.split(key, 32)
    f32 = jnp.float32
    L = DEPTH

    def nrm(k, shape, fan_in):
        return jax.random.normal(k, shape, f32) * (fan_in ** -0.5)

    def gain(k, shape):
        return 1.0 + 0.02 * jax.random.normal(k, shape, f32)

    def bias(k, shape):
        return 0.02 * jax.random.normal(k, shape, f32)

    dt = jnp.exp(jax.random.uniform(ks[5], (L, GDN_HEADS), f32, math.log(1e-3), math.log(1e-1)))
    return {
        'x': jax.random.normal(ks[0], (BATCH, SEQ, D_MODEL), f32),
        'mem': jax.random.normal(ks[1], (BATCH, MEM_LEN, D_MODEL), f32),
        'norm_mix': gain(ks[2], (L, D_MODEL)),
        'w_in': nrm(ks[3], (L, D_MODEL, IN_DIM), D_MODEL),
        'gdn_conv_w': nrm(ks[4], (L, GDN_CONV, QKV_A), GDN_CONV),
        'gdn_dt_bias': dt + jnp.log(-jnp.expm1(-dt)),
        'gdn_a_log': jnp.log(jax.random.uniform(ks[6], (L, GDN_HEADS), f32, 1.0, 16.0)),
        'gdn_norm': gain(ks[7], (L, GDN_DV)),
        'w_gdn_out': nrm(ks[8], (L, V_A, D_MODEL), V_A),
        'cc_glu_b': bias(ks[9], (L, 2 * CONV_CH)),
        'cc_dw_w': nrm(ks[10], (L, CONV_K, CONV_CH), CONV_K),
        'cc_dw_b': bias(ks[11], (L, CONV_CH)),
        'cc_ln_w': gain(ks[12], (L, CONV_CH)),
        'cc_ln_b': bias(ks[13], (L, CONV_CH)),
        'w_cc_out': nrm(ks[14], (L, CONV_CH, D_MODEL), CONV_CH),
        'mem_norm': gain(ks[15], (L, D_MODEL)),
        'w_mem_kv': nrm(ks[16], (L, D_MODEL, 2 * XA_W), D_MODEL),
        'w_xa_out': nrm(ks[17], (L, XA_W, D_MODEL), XA_W),
        'gate_b': bias(ks[18], (L, N_BRANCH * D_MODEL)),
        'w_o': nrm(ks[19], (L, D_MODEL, D_MODEL), D_MODEL),
        'norm_ffn': gain(ks[20], (L, D_MODEL)),
        'w_up': nrm(ks[21], (L, D_MODEL, 2 * FFN_DIM), D_MODEL),
        'ffn_dw_w': nrm(ks[22], (L, FFN_CONV, FFN_DIM), FFN_CONV),
        'ffn_dw_b': bias(ks[23], (L, FFN_DIM)),
        'w_down': nrm(ks[24], (L, FFN_DIM, D_MODEL), FFN_DIM),
        'norm_final': gain(ks[25], (D_MODEL,)),
    }


def reference(x, mem, norm_mix, w_in, gdn_conv_w, gdn_dt_bias, gdn_a_log, gdn_norm, w_gdn_out,
              cc_glu_b, cc_dw_w, cc_dw_b, cc_ln_w, cc_ln_b, w_cc_out,
              mem_norm, w_mem_kv, w_xa_out, gate_b, w_o,
              norm_ffn, w_up, ffn_dw_w, ffn_dw_b, w_down, norm_final):
    Bn, Sn, D = x.shape
    Mn = mem.shape[1]
    dt = x.dtype
    f32 = jnp.float32
    for l in range(DEPTH):
        h = rmsnorm(x, norm_mix[l])
        proj = h @ w_in[l]
        qkv_a, a_a, b_a, z_a, glu_in, q_c, gate_logits = jnp.split(proj, _split_points(), axis=-1)

        qkv_a = jax.nn.silu(causal_dwconv(qkv_a, gdn_conv_w[l]))
        q_a, k_a, v_a = jnp.split(qkv_a, [QK_A, 2 * QK_A], axis=-1)
        q_a = l2norm(q_a.reshape(Bn, Sn, GDN_HEADS, GDN_DK))
        k_a = l2norm(k_a.reshape(Bn, Sn, GDN_HEADS, GDN_DK))
        v_a = v_a.reshape(Bn, Sn, GDN_HEADS, GDN_DV)
        g_a = -jnp.exp(gdn_a_log[l].astype(f32)) * jax.nn.softplus(a_a.astype(f32) + gdn_dt_bias[l].astype(f32))
        beta_a = jax.nn.sigmoid(b_a.astype(f32))
        o_a = gated_delta_rule(q_a, k_a, v_a, g_a, beta_a).astype(dt)
        o_a = rmsnorm(o_a, gdn_norm[l]) * jax.nn.silu(z_a.reshape(Bn, Sn, GDN_HEADS, GDN_DV))
        y_a = o_a.reshape(Bn, Sn, V_A) @ w_gdn_out[l]

        glu = glu_in + cc_glu_b[l]
        u = glu[..., :CONV_CH] * jax.nn.sigmoid(glu[..., CONV_CH:])
        u = causal_dwconv(u, cc_dw_w[l]) + cc_dw_b[l]
        u = jax.nn.silu(layernorm(u, cc_ln_w[l], cc_ln_b[l]))
        y_b = u @ w_cc_out[l]

        kv_m = rmsnorm(mem, mem_norm[l]) @ w_mem_kv[l]
        k_m = kv_m[..., :XA_W].reshape(Bn, Mn, XA_HEADS, XA_DH)
        v_m = kv_m[..., XA_W:].reshape(Bn, Mn, XA_HEADS, XA_DH)
        q_m = q_c.reshape(Bn, Sn, XA_HEADS, XA_DH)
        s = jnp.einsum('bshd,bmhd->bhsm', q_m, k_m).astype(f32) * (XA_DH ** -0.5)
        p = jax.nn.softmax(s, axis=-1).astype(dt)
        o_c = jnp.einsum('bhsm,bmhd->bshd', p, v_m).reshape(Bn, Sn, XA_W)
        y_c = o_c @ w_xa_out[l]

        gates = jax.nn.sigmoid((gate_logits + gate_b[l]).astype(f32)).astype(dt).reshape(Bn, Sn, N_BRANCH, D)
        merged = gates[..., 0, :] * y_a + gates[..., 1, :] * y_b + gates[..., 2, :] * y_c
        x = x + merged @ w_o[l]

        h = rmsnorm(x, norm_ffn[l])
        up = h @ w_up[l]
        g_f = causal_dwconv(up[..., :FFN_DIM], ffn_dw_w[l]) + ffn_dw_b[l]
        x = x + (jax.nn.silu(g_f) * up[..., FFN_DIM:]) @ w_down[l]
    return rmsnorm(x, norm_final)
```

```python
import functools

import jax
import jax.numpy as jnp
from jax import lax
from jax.experimental import pallas as pl
from jax.experimental.pallas import tpu as pltpu

EPS = 1e-6
GDN_HEADS = 4
GDN_DK = 128
GDN_DV = 128
GDN_CONV = 4
GDN_CHUNK = 64
QK_A = GDN_HEADS * GDN_DK
V_A = GDN_HEADS * GDN_DV
QKV_A = 2 * QK_A + V_A
CONV_CH = 512
CONV_K = 31
XA_HEADS = 4
XA_DH = 128
XA_W = XA_HEADS * XA_DH
N_BRANCH = 3
FFN_CONV = 3

LANES = 128
SUBLANES = 8
ROW_TILE = 256
CONV_ROWS = 64
QKV_HALO = SUBLANES
CC_HALO = 4 * SUBLANES
FFN_HALO = SUBLANES
VMEM_LIMIT_BYTES = 56 * 1024 * 1024

BF16 = jnp.bfloat16
F32 = jnp.float32


def _mm(a, b):
    return jnp.dot(a.astype(BF16), b.astype(BF16), preferred_element_type=F32)


def _mm_nt(a, b):
    return lax.dot_general(a.astype(BF16), b.astype(BF16), (((1,), (1,)), ((), ())),
                           preferred_element_type=F32)


def _mm_tn(a, b):
    return lax.dot_general(a.astype(BF16), b.astype(BF16), (((0,), (0,)), ((), ())),
                           preferred_element_type=F32)


def _rmsnorm(x, w_row):
    ms = jnp.mean(x * x, axis=-1, keepdims=True)
    return x * lax.rsqrt(ms + EPS) * w_row


def _sigmoid(x):
    return 1.0 / (1.0 + jnp.exp(-x))


def _silu(x):
    return x * _sigmoid(x)


def _softplus(x):
    return jnp.maximum(x, 0.0) + jnp.log(1.0 + jnp.exp(-jnp.abs(x)))


def _causal_dwconv(buf_ref, w_ref, halo, n_rows, n_taps, emit):
    for c0 in range(0, buf_ref.shape[1], LANES):
        for r0 in range(0, n_rows, CONV_ROWS):
            acc = None
            for k in range(n_taps):
                start = halo - (n_taps - 1) + k + r0
                term = w_ref[k:k + 1, c0:c0 + LANES] * buf_ref[start:start + CONV_ROWS, c0:c0 + LANES]
                acc = term if acc is None else acc + term
            emit(r0, c0, acc)


def _memkv_kernel(mem_ref, nrm_ref, w_ref, out_ref):
    m = _rmsnorm(mem_ref[0], nrm_ref[0])
    out_ref[0, 0] = _mm(m, w_ref[0]).astype(out_ref.dtype)


def _memkv(mem, mem_norm, w_mem_kv):
    n_layers = w_mem_kv.shape[0]
    bn, mn, d = mem.shape
    n_out = w_mem_kv.shape[2]
    return pl.pallas_call(
        _memkv_kernel,
        grid=(n_layers, bn),
        in_specs=[
            pl.BlockSpec((1, mn, d), lambda l, b: (b, 0, 0)),
            pl.BlockSpec((1, 1, d), lambda l, b: (l, 0, 0)),
            pl.BlockSpec((1, d, n_out), lambda l, b: (l, 0, 0)),
        ],
        out_specs=pl.BlockSpec((1, 1, mn, n_out), lambda l, b: (l, b, 0, 0)),
        out_shape=jax.ShapeDtypeStruct((n_layers, bn, mn, n_out), BF16),
        compiler_params=pltpu.CompilerParams(
            dimension_semantics=("arbitrary", "arbitrary"),
            vmem_limit_bytes=VMEM_LIMIT_BYTES),
        name="mem_kv",
    )(mem, mem_norm.reshape(n_layers, 1, d), w_mem_kv.astype(BF16))


def _mixer_kernel(x_ref, kvm_ref, nmix_ref, wqkv_ref, wab_ref, wz_ref, wglu_ref, wqc_ref,
                  wgate_ref, convw_ref, alog_ref, dtb_ref, gnorm_ref, wgdn_ref,
                  glub_ref, dww_ref, dwb_ref, lnw_ref, lnb_ref, wcc_ref,
                  wxa_ref, gateb_ref, wo_ref,
                  out_ref,
                  h_s, qkv_s, q_s, k_s, v_s, g_s, beta_s, o_s, ubuf_s, conv_s, state_s, merged_s):
    tm = x_ref.shape[1]
    d = x_ref.shape[2]

    @pl.when(pl.program_id(1) == 0)
    def _():
        qkv_s[0:QKV_HALO, :] = jnp.zeros((QKV_HALO, QKV_A), F32)
        ubuf_s[0:CC_HALO, :] = jnp.zeros((CC_HALO, CONV_CH), F32)
        state_s[...] = jnp.zeros(state_s.shape, F32)

    x = x_ref[0]
    h_s[...] = _rmsnorm(x, nmix_ref[...]).astype(BF16)

    qkv_s[QKV_HALO:QKV_HALO + tm, :] = _mm(h_s[...], wqkv_ref[...])

    def emit_qkv(r0, c0, blk):
        blk = _silu(blk)
        rows = slice(r0, r0 + CONV_ROWS)
        if c0 < 2 * QK_A:
            blk = blk * lax.rsqrt(jnp.sum(blk * blk, axis=-1, keepdims=True) + EPS)
        if c0 < QK_A:
            q_s[rows, c0:c0 + LANES] = blk * (GDN_DK ** -0.5)
        elif c0 < 2 * QK_A:
            k_s[rows, c0 - QK_A:c0 - QK_A + LANES] = blk
        else:
            v_s[rows, c0 - 2 * QK_A:c0 - 2 * QK_A + LANES] = blk

    _causal_dwconv(qkv_s, convw_ref, QKV_HALO, tm, GDN_CONV, emit_qkv)
    qkv_s[0:QKV_HALO, :] = qkv_s[tm:tm + QKV_HALO, :]

    ab = _mm(h_s[...], wab_ref[...])
    g_s[...] = -jnp.exp(alog_ref[...]) * _softplus(ab + dtb_ref[...])
    beta_s[...] = _sigmoid(ab)

    ri = lax.broadcasted_iota(jnp.int32, (GDN_CHUNK, GDN_CHUNK), 0)
    ci = lax.broadcasted_iota(jnp.int32, (GDN_CHUNK, GDN_CHUNK), 1)
    causal = ri >= ci
    strict = ri > ci
    tri = causal.astype(F32)
    eye = (ri == ci).astype(F32)

    def chunk_body(c, carry):
        r0 = pl.multiple_of(c * GDN_CHUNK, GDN_CHUNK)
        rows = pl.ds(r0, GDN_CHUNK)
        gc = jnp.dot(tri, g_s[rows, :], precision=lax.Precision.HIGHEST,
                     preferred_element_type=F32)
        gl = gc[GDN_CHUNK - 1:GDN_CHUNK, :]
        eg = jnp.exp(gc)
        egl = jnp.exp(gl - gc)
        eg_last = jnp.exp(gl)
        gct = jnp.concatenate([gc, jnp.zeros_like(gc)], axis=0).T
        beta_c = beta_s[rows, :]
        for hh in range(GDN_HEADS):
            ls = slice(hh * GDN_DK, (hh + 1) * GDN_DK)
            q_h = q_s[rows, ls]
            k_h = k_s[rows, ls]
            v_h = v_s[rows, ls]
            gcol = gc[:, hh:hh + 1]
            grow = gct[hh:hh + 1, 0:GDN_CHUNK]
            bcol = beta_c[:, GDN_HEADS + hh:GDN_HEADS + hh + 1]
            decay = jnp.exp(jnp.where(causal, gcol - grow, -jnp.inf))
            kb = k_h * bcol
            a_mat = jnp.where(strict, _mm_nt(kb, k_h) * decay, 0.0)
            t_mat = eye - a_mat
            a_pow = a_mat
            for _ in range(5):
                a_pow = _mm(a_pow, a_pow)
                t_mat = t_mat + _mm(t_mat, a_pow)
            u = _mm(t_mat, v_h * bcol)
            w = _mm(t_mat, kb * eg[:, hh:hh + 1])
            qk = jnp.where(causal, _mm_nt(q_h, k_h) * decay, 0.0)
            q_dec = q_h * eg[:, hh:hh + 1]
            k_dec = k_h * egl[:, hh:hh + 1]
            st = state_s[hh]
            v_new = u - _mm(w, st)
            o_s[rows, ls] = _mm(q_dec, st) + _mm(qk, v_new)
            state_s[hh] = st * eg_last[:, hh:hh + 1] + _mm_tn(k_dec, v_new)
        return carry

    lax.fori_loop(0, tm // GDN_CHUNK, chunk_body, 0)

    z = _mm(h_s[...], wz_ref[...])
    for hh in range(GDN_HEADS):
        ls = slice(hh * GDN_DV, (hh + 1) * GDN_DV)
        o_s[:, ls] = _rmsnorm(o_s[:, ls], gnorm_ref[...]) * _silu(z[:, ls])
    y = _mm(o_s[...], wgdn_ref[...])
    gate = _sigmoid(_mm(h_s[...], wgate_ref[:, 0:d]) + gateb_ref[:, 0:d])
    merged_s[...] = gate * y

    glu = _mm(h_s[...], wglu_ref[...]) + glub_ref[...]
    ubuf_s[CC_HALO:CC_HALO + tm, :] = glu[:, :CONV_CH] * _sigmoid(glu[:, CONV_CH:])

    def emit_cc(r0, c0, blk):
        conv_s[r0:r0 + CONV_ROWS, c0:c0 + LANES] = blk + dwb_ref[:, c0:c0 + LANES]

    _causal_dwconv(ubuf_s, dww_ref, CC_HALO, tm, CONV_K, emit_cc)
    ubuf_s[0:CC_HALO, :] = ubuf_s[tm:tm + CC_HALO, :]
    for r0 in range(0, tm, CONV_ROWS):
        blk = conv_s[r0:r0 + CONV_ROWS, :]
        mu = jnp.mean(blk, axis=-1, keepdims=True)
        cen = blk - mu
        var = jnp.mean(cen * cen, axis=-1, keepdims=True)
        conv_s[r0:r0 + CONV_ROWS, :] = _silu(cen * lax.rsqrt(var + EPS) * lnw_ref[...] + lnb_ref[...])
    y = _mm(conv_s[...], wcc_ref[...])
    gate = _sigmoid(_mm(h_s[...], wgate_ref[:, d:2 * d]) + gateb_ref[:, d:2 * d])
    merged_s[...] += gate * y

    qc = _mm(h_s[...], wqc_ref[...])
    for hh in range(XA_HEADS):
        ls = slice(hh * XA_DH, (hh + 1) * XA_DH)
        k_m = kvm_ref[0, :, hh * XA_DH:(hh + 1) * XA_DH]
        v_m = kvm_ref[0, :, XA_W + hh * XA_DH:XA_W + (hh + 1) * XA_DH]
        sc = _mm_nt(qc[:, ls], k_m) * (XA_DH ** -0.5)
        ex = jnp.exp(sc - jnp.max(sc, axis=-1, keepdims=True))
        p = ex / jnp.sum(ex, axis=-1, keepdims=True)
        conv_s[:, ls] = _mm(p, v_m)
    y = _mm(conv_s[...], wxa_ref[...])
    gate = _sigmoid(_mm(h_s[...], wgate_ref[:, 2 * d:3 * d]) + gateb_ref[:, 2 * d:3 * d])
    merged_s[...] += gate * y

    out_ref[0] = x + _mm(merged_s[...], wo_ref[...])


def _const_spec(shape):
    nd = len(shape)
    return pl.BlockSpec(shape, lambda b, s: (0,) * nd, pipeline_mode=pl.Buffered(1))


def _mixer(x, kvm, norm_mix, w_in, gdn_conv_w, gdn_dt_bias, gdn_a_log, gdn_norm, w_gdn_out,
           cc_glu_b, cc_dw_w, cc_dw_b, cc_ln_w, cc_ln_b, w_cc_out, w_xa_out, gate_b, w_o):
    bn, sn, d = x.shape
    tm = ROW_TILE
    p0 = QKV_A
    p1 = p0 + GDN_HEADS
    p2 = p1 + GDN_HEADS
    p3 = p2 + V_A
    p4 = p3 + 2 * CONV_CH
    p5 = p4 + XA_W
    w_qkv = w_in[:, :p0].astype(BF16)
    w_ab = jnp.pad(w_in[:, p0:p2], ((0, 0), (0, LANES - 2 * GDN_HEADS))).astype(BF16)
    w_z = w_in[:, p2:p3].astype(BF16)
    w_glu = w_in[:, p3:p4].astype(BF16)
    w_qc = w_in[:, p4:p5].astype(BF16)
    w_gate = w_in[:, p5:].astype(BF16)
    pad_row = lambda v: jnp.pad(v, (0, LANES - v.shape[0])).reshape(1, LANES)
    row = lambda v: v.reshape(1, -1)
    consts = [
        row(norm_mix), w_qkv, w_ab, w_z, w_glu, w_qc, w_gate,
        gdn_conv_w, pad_row(gdn_a_log), pad_row(gdn_dt_bias), row(gdn_norm), w_gdn_out.astype(BF16),
        row(cc_glu_b), cc_dw_w, row(cc_dw_b), row(cc_ln_w), row(cc_ln_b), w_cc_out.astype(BF16),
        w_xa_out.astype(BF16), row(gate_b), w_o.astype(BF16),
    ]
    mn, kvw = kvm.shape[1], kvm.shape[2]
    return pl.pallas_call(
        _mixer_kernel,
        grid=(bn, sn // tm),
        in_specs=[pl.BlockSpec((1, tm, d), lambda b, s: (b, s, 0)),
                  pl.BlockSpec((1, mn, kvw), lambda b, s: (b, 0, 0))]
                 + [_const_spec(c.shape) for c in consts],
        out_specs=pl.BlockSpec((1, tm, d), lambda b, s: (b, s, 0)),
        out_shape=jax.ShapeDtypeStruct(x.shape, x.dtype),
        scratch_shapes=[
            pltpu.VMEM((tm, d), BF16),
            pltpu.VMEM((QKV_HALO + tm, QKV_A), F32),
            pltpu.VMEM((tm, QK_A), F32),
            pltpu.VMEM((tm, QK_A), F32),
            pltpu.VMEM((tm, V_A), F32),
            pltpu.VMEM((tm, LANES), F32),
            pltpu.VMEM((tm, LANES), F32),
            pltpu.VMEM((tm, V_A), F32),
            pltpu.VMEM((CC_HALO + tm, CONV_CH), F32),
            pltpu.VMEM((tm, CONV_CH), F32),
            pltpu.VMEM((GDN_HEADS, GDN_DK, GDN_DV), F32),
            pltpu.VMEM((tm, d), F32),
        ],
        compiler_params=pltpu.CompilerParams(
            dimension_semantics=("arbitrary", "arbitrary"),
            vmem_limit_bytes=VMEM_LIMIT_BYTES),
        name="mixer",
    )(x, kvm, *consts)


def _ffn_kernel(x_ref, nffn_ref, wup_ref, dww_ref, dwb_ref, wdown_ref, nfin_ref, out_ref,
                gbuf_s, act_s, *, final):
    tm = x_ref.shape[1]
    f = dww_ref.shape[1]

    @pl.when(pl.program_id(1) == 0)
    def _():
        gbuf_s[0:FFN_HALO, :] = jnp.zeros((FFN_HALO, f), F32)

    x = x_ref[0]
    hb = _rmsnorm(x, nffn_ref[...]).astype(BF16)
    gbuf_s[FFN_HALO:FFN_HALO + tm, :] = _mm(hb, wup_ref[:, :f])
    act_s[...] = _mm(hb, wup_ref[:, f:])

    def emit(r0, c0, blk):
        rows, cols = slice(r0, r0 + CONV_ROWS), slice(c0, c0 + LANES)
        act_s[rows, cols] = _silu(blk + dwb_ref[:, cols]) * act_s[rows, cols]

    _causal_dwconv(gbuf_s, dww_ref, FFN_HALO, tm, FFN_CONV, emit)
    gbuf_s[0:FFN_HALO, :] = gbuf_s[tm:tm + FFN_HALO, :]
    y = x + _mm(act_s[...], wdown_ref[...])
    if final:
        y = _rmsnorm(y, nfin_ref[...])
    out_ref[0] = y


def _ffn(x, norm_ffn, w_up, ffn_dw_w, ffn_dw_b, w_down, norm_final, final):
    bn, sn, d = x.shape
    tm = ROW_TILE
    f = w_down.shape[0]
    consts = [norm_ffn.reshape(1, d), w_up.astype(BF16), ffn_dw_w, ffn_dw_b.reshape(1, f),
              w_down.astype(BF16), norm_final.reshape(1, d)]
    return pl.pallas_call(
        functools.partial(_ffn_kernel, final=final),
        grid=(bn, sn // tm),
        in_specs=[pl.BlockSpec((1, tm, d), lambda b, s: (b, s, 0))]
                 + [_const_spec(c.shape) for c in consts],
        out_specs=pl.BlockSpec((1, tm, d), lambda b, s: (b, s, 0)),
        out_shape=jax.ShapeDtypeStruct(x.shape, x.dtype),
        scratch_shapes=[
            pltpu.VMEM((FFN_HALO + tm, f), F32),
            pltpu.VMEM((tm, f), F32),
        ],
        compiler_params=pltpu.CompilerParams(
            dimension_semantics=("arbitrary", "arbitrary"),
            vmem_limit_bytes=VMEM_LIMIT_BYTES),
        name="ffn",
    )(x, *consts)


def kernel(x, mem, norm_mix, w_in, gdn_conv_w, gdn_dt_bias, gdn_a_log, gdn_norm, w_gdn_out, cc_glu_b, cc_dw_w, cc_dw_b, cc_ln_w, cc_ln_b, w_cc_out, mem_norm, w_mem_kv, w_xa_out, gate_b, w_o, norm_ffn, w_up, ffn_dw_w, ffn_dw_b, w_down, norm_final):
    n_layers = w_in.shape[0]
    assert x.shape[1] % ROW_TILE == 0 and ROW_TILE % GDN_CHUNK == 0 and ROW_TILE % CONV_ROWS == 0
    kvm = _memkv(mem, mem_norm, w_mem_kv)
    for l in range(n_layers):
        x = _mixer(x, kvm[l], norm_mix[l], w_in[l], gdn_conv_w[l], gdn_dt_bias[l], gdn_a_log[l],
                   gdn_norm[l], w_gdn_out[l], cc_glu_b[l], cc_dw_w[l], cc_dw_b[l], cc_ln_w[l],
                   cc_ln_b[l], w_cc_out[l], w_xa_out[l], gate_b[l], w_o[l])
        x = _ffn(x, norm_ffn[l], w_up[l], ffn_dw_w[l], ffn_dw_b[l], w_down[l], norm_final,
                 final=(l == n_layers - 1))
    return x
```

```python
import functools

import numpy as np
import jax
import jax.numpy as jnp
from jax import lax
from jax.experimental import pallas as pl
from jax.experimental.pallas import tpu as pltpu

EPS = 1e-6
GDN_HEADS = 4
GDN_DK = 128
GDN_DV = 128
GDN_CONV = 4
GDN_CHUNK = 64
QK_A = GDN_HEADS * GDN_DK
V_A = GDN_HEADS * GDN_DV
QKV_A = 2 * QK_A + V_A
CONV_CH = 512
CONV_K = 31
XA_HEADS = 4
XA_DH = 128
XA_W = XA_HEADS * XA_DH
N_BRANCH = 3
FFN_CONV = 3

LANES = 128
SUBLANES = 8
ROW_TILE = 256
CONV_ROWS = 64
QKV_HALO = SUBLANES
CC_HALO = 4 * SUBLANES
FFN_HALO = SUBLANES
VMEM_LIMIT_BYTES = 56 * 1024 * 1024

BF16 = jnp.bfloat16
F32 = jnp.float32


def _mm(a, b):
    return jnp.dot(a.astype(BF16), b.astype(BF16), preferred_element_type=F32)


def _mm_nt(a, b):
    return lax.dot_general(a.astype(BF16), b.astype(BF16), (((1,), (1,)), ((), ())),
                           preferred_element_type=F32)


def _mm_tn(a, b):
    return lax.dot_general(a.astype(BF16), b.astype(BF16), (((0,), (0,)), ((), ())),
                           preferred_element_type=F32)


def _rmsnorm(x, w_row):
    ms = jnp.mean(x * x, axis=-1, keepdims=True)
    return x * lax.rsqrt(ms + EPS) * w_row


def _sigmoid(x):
    return 1.0 / (1.0 + jnp.exp(-x))


def _silu(x):
    return x * _sigmoid(x)


def _softplus(x):
    return jnp.maximum(x, 0.0) + jnp.log(1.0 + jnp.exp(-jnp.abs(x)))


def _causal_dwconv(buf_ref, w_ref, halo, n_rows, n_taps, emit):
    for c0 in range(0, buf_ref.shape[1], LANES):
        for r0 in range(0, n_rows, CONV_ROWS):
            acc = None
            for k in range(n_taps):
                start = halo - (n_taps - 1) + k + r0
                term = w_ref[k:k + 1, c0:c0 + LANES] * buf_ref[start:start + CONV_ROWS, c0:c0 + LANES]
                acc = term if acc is None else acc + term
            emit(r0, c0, acc)


def _memkv_kernel(mem_ref, nrm_ref, w_ref, out_ref):
    m = _rmsnorm(mem_ref[0], nrm_ref[0])
    out_ref[0, 0] = _mm(m, w_ref[0]).astype(out_ref.dtype)


def _memkv(mem, mem_norm, w_mem_kv):
    n_layers = w_mem_kv.shape[0]
    bn, mn, d = mem.shape
    n_out = w_mem_kv.shape[2]
    return pl.pallas_call(
        _memkv_kernel,
        grid=(n_layers, bn),
        in_specs=[
            pl.BlockSpec((1, mn, d), lambda l, b: (b, 0, 0)),
            pl.BlockSpec((1, 1, d), lambda l, b: (l, 0, 0)),
            pl.BlockSpec((1, d, n_out), lambda l, b: (l, 0, 0)),
        ],
        out_specs=pl.BlockSpec((1, 1, mn, n_out), lambda l, b: (l, b, 0, 0)),
        out_shape=jax.ShapeDtypeStruct((n_layers, bn, mn, n_out), BF16),
        compiler_params=pltpu.CompilerParams(
            dimension_semantics=("arbitrary", "arbitrary"),
            vmem_limit_bytes=VMEM_LIMIT_BYTES),
        name="mem_kv",
    )(mem, mem_norm.reshape(n_layers, 1, d), w_mem_kv.astype(BF16))


def _mixer_kernel(x_ref, kvm_ref, nmix_ref, wqkv_ref, wab_ref, wz_ref, wglu_ref, wqc_ref,
                  wgate_ref, convw_ref, alog_ref, dtb_ref, gnorm_ref, wgdn_ref,
                  glub_ref, dww_ref, dwb_ref, lnw_ref, lnb_ref, wcc_ref,
                  wxa_ref, gateb_ref, wo_ref, lvl_ref,
                  out_ref,
                  h_s, qkv_s, q_s, k_s, v_s, g_s, beta_s, o_s, ubuf_s, conv_s, state_s, merged_s):
    tm = x_ref.shape[1]
    d = x_ref.shape[2]

    @pl.when(pl.program_id(1) == 0)
    def _():
        qkv_s[0:QKV_HALO, :] = jnp.zeros((QKV_HALO, QKV_A), F32)
        ubuf_s[0:CC_HALO, :] = jnp.zeros((CC_HALO, CONV_CH), F32)
        state_s[...] = jnp.zeros(state_s.shape, F32)

    x = x_ref[0]
    h_s[...] = _rmsnorm(x, nmix_ref[...]).astype(BF16)

    qkv_s[QKV_HALO:QKV_HALO + tm, :] = _mm(h_s[...], wqkv_ref[...])

    def emit_qkv(r0, c0, blk):
        blk = _silu(blk)
        rows = slice(r0, r0 + CONV_ROWS)
        if c0 < 2 * QK_A:
            blk = blk * lax.rsqrt(jnp.sum(blk * blk, axis=-1, keepdims=True) + EPS)
        if c0 < QK_A:
            q_s[rows, c0:c0 + LANES] = blk * (GDN_DK ** -0.5)
        elif c0 < 2 * QK_A:
            k_s[rows, c0 - QK_A:c0 - QK_A + LANES] = blk
        else:
            v_s[rows, c0 - 2 * QK_A:c0 - 2 * QK_A + LANES] = blk

    _causal_dwconv(qkv_s, convw_ref, QKV_HALO, tm, GDN_CONV, emit_qkv)
    qkv_s[0:QKV_HALO, :] = qkv_s[tm:tm + QKV_HALO, :]

    ab = _mm(h_s[...], wab_ref[...])
    g_s[...] = -jnp.exp(alog_ref[...]) * _softplus(ab + dtb_ref[...])
    beta_s[...] = _sigmoid(ab)

    n_chunks = tm // GDN_CHUNK
    chunk_bits = GDN_CHUNK.bit_length() - 1
    ri = lax.broadcasted_iota(jnp.int32, (tm, tm), 0)
    ci = lax.broadcasted_iota(jnp.int32, (tm, tm), 1)
    same_chunk = (ri >> chunk_bits) == (ci >> chunk_bits)
    causal = same_chunk & (ri >= ci)
    strict = same_chunk & (ri > ci)
    tri = causal.astype(BF16)
    eye = (ri == ci).astype(F32)

    g = g_s[...]
    g_hi = g.astype(BF16)
    g_r1 = g - g_hi.astype(F32)
    g_mid = g_r1.astype(BF16)
    g_lo = (g_r1 - g_mid.astype(F32)).astype(BF16)
    gc = _mm(tri, g_hi) + _mm(tri, g_mid) + _mm(tri, g_lo)
    gct = gc.T
    eg = jnp.exp(gc)
    egl_parts, eg_last = [], []
    for c in range(n_chunks):
        gl = gc[(c + 1) * GDN_CHUNK - 1:(c + 1) * GDN_CHUNK, :]
        egl_parts.append(jnp.exp(gl - gc[c * GDN_CHUNK:(c + 1) * GDN_CHUNK, :]))
        eg_last.append(jnp.exp(gl))
    egl = jnp.concatenate(egl_parts, axis=0)
    beta = beta_s[...]

    heads = range(GDN_HEADS)
    lanes = [slice(hh * GDN_DK, (hh + 1) * GDN_DK) for hh in heads]
    q_h = [q_s[:, ls] for ls in lanes]
    k_h = [k_s[:, ls] for ls in lanes]
    bcol = [beta[:, GDN_HEADS + hh:GDN_HEADS + hh + 1] for hh in heads]
    egcol = [eg[:, hh:hh + 1] for hh in heads]
    kb = [k_h[hh] * bcol[hh] for hh in heads]
    decay = [jnp.exp(jnp.where(causal, gc[:, hh:hh + 1] - gct[hh:hh + 1, :], -jnp.inf)) for hh in heads]
    a_mat = [jnp.where(strict, _mm_nt(kb[hh], k_h[hh]) * decay[hh], 0.0) for hh in heads]
    qk = [jnp.where(causal, _mm_nt(q_h[hh], k_h[hh]) * decay[hh], 0.0) for hh in heads]
    t_mat = [eye - a_mat[hh] * lvl_ref[0] for hh in heads]
    for j in range(1, chunk_bits):
        xt = [_mm(a_mat[hh] * lvl_ref[j], t_mat[hh]) for hh in heads]
        t_mat = [t_mat[hh] - _mm(t_mat[hh], xt[hh]) for hh in heads]
    uw = [_mm(t_mat[hh], jnp.concatenate([v_s[:, lanes[hh]] * bcol[hh], kb[hh] * egcol[hh]], axis=1))
          for hh in heads]
    loc = [_mm(qk[hh], uw[hh]) for hh in heads]
    q_til = [q_h[hh] * egcol[hh] - loc[hh][:, GDN_DV:] for hh in heads]
    k_dec = [k_h[hh] * egl[:, hh:hh + 1] for hh in heads]
    chunk_rows = [slice(c * GDN_CHUNK, (c + 1) * GDN_CHUNK) for c in range(n_chunks)]
    kuw = [[_mm_tn(k_dec[hh][rows], uw[hh][rows]) for rows in chunk_rows] for hh in heads]
    st = [state_s[hh] for hh in heads]
    o_parts = [[] for _ in heads]
    for c, rows in enumerate(chunk_rows):
        for hh in heads:
            r = _mm(jnp.concatenate([kuw[hh][c][:, GDN_DV:], q_til[hh][rows]], axis=0), st[hh])
            o_parts[hh].append(r[GDN_DK:])
            st[hh] = st[hh] * eg_last[c][:, hh:hh + 1] + kuw[hh][c][:, :GDN_DV] - r[:GDN_DK]
    for hh in heads:
        state_s[hh] = st[hh]
        o_s[:, lanes[hh]] = jnp.concatenate(o_parts[hh], axis=0) + loc[hh][:, :GDN_DV]

    z = _mm(h_s[...], wz_ref[...])
    for hh in range(GDN_HEADS):
        ls = slice(hh * GDN_DV, (hh + 1) * GDN_DV)
        o_s[:, ls] = _rmsnorm(o_s[:, ls], gnorm_ref[...]) * _silu(z[:, ls])
    y = _mm(o_s[...], wgdn_ref[...])
    gate = _sigmoid(_mm(h_s[...], wgate_ref[:, 0:d]) + gateb_ref[:, 0:d])
    merged_s[...] = gate * y

    glu = _mm(h_s[...], wglu_ref[...]) + glub_ref[...]
    ubuf_s[CC_HALO:CC_HALO + tm, :] = glu[:, :CONV_CH] * _sigmoid(glu[:, CONV_CH:])

    def emit_cc(r0, c0, blk):
        conv_s[r0:r0 + CONV_ROWS, c0:c0 + LANES] = blk + dwb_ref[:, c0:c0 + LANES]

    _causal_dwconv(ubuf_s, dww_ref, CC_HALO, tm, CONV_K, emit_cc)
    ubuf_s[0:CC_HALO, :] = ubuf_s[tm:tm + CC_HALO, :]
    for r0 in range(0, tm, CONV_ROWS):
        blk = conv_s[r0:r0 + CONV_ROWS, :]
        mu = jnp.mean(blk, axis=-1, keepdims=True)
        cen = blk - mu
        var = jnp.mean(cen * cen, axis=-1, keepdims=True)
        conv_s[r0:r0 + CONV_ROWS, :] = _silu(cen * lax.rsqrt(var + EPS) * lnw_ref[...] + lnb_ref[...])
    y = _mm(conv_s[...], wcc_ref[...])
    gate = _sigmoid(_mm(h_s[...], wgate_ref[:, d:2 * d]) + gateb_ref[:, d:2 * d])
    merged_s[...] += gate * y

    qc = _mm(h_s[...], wqc_ref[...])
    for hh in range(XA_HEADS):
        ls = slice(hh * XA_DH, (hh + 1) * XA_DH)
        k_m = kvm_ref[0, :, hh * XA_DH:(hh + 1) * XA_DH]
        v_m = kvm_ref[0, :, XA_W + hh * XA_DH:XA_W + (hh + 1) * XA_DH]
        sc = _mm_nt(qc[:, ls], k_m) * (XA_DH ** -0.5)
        ex = jnp.exp(sc - jnp.max(sc, axis=-1, keepdims=True))
        p = ex / jnp.sum(ex, axis=-1, keepdims=True)
        conv_s[:, ls] = _mm(p, v_m)
    y = _mm(conv_s[...], wxa_ref[...])
    gate = _sigmoid(_mm(h_s[...], wgate_ref[:, 2 * d:3 * d]) + gateb_ref[:, 2 * d:3 * d])
    merged_s[...] += gate * y

    out_ref[0] = x + _mm(merged_s[...], wo_ref[...])


def _level_masks(tm):
    r = np.arange(tm)[:, None]
    c = np.arange(tm)[None, :]
    same_chunk = (r // GDN_CHUNK) == (c // GDN_CHUNK)
    n_levels = GDN_CHUNK.bit_length() - 1
    masks = [same_chunk & (r > c) & (((r ^ c) >> j) == 1) for j in range(n_levels)]
    return jnp.asarray(np.stack(masks).astype(np.float32))


def _const_spec(shape):
    nd = len(shape)
    return pl.BlockSpec(shape, lambda b, s: (0,) * nd, pipeline_mode=pl.Buffered(1))


def _mixer(x, kvm, norm_mix, w_in, gdn_conv_w, gdn_dt_bias, gdn_a_log, gdn_norm, w_gdn_out,
           cc_glu_b, cc_dw_w, cc_dw_b, cc_ln_w, cc_ln_b, w_cc_out, w_xa_out, gate_b, w_o):
    bn, sn, d = x.shape
    tm = ROW_TILE
    p0 = QKV_A
    p1 = p0 + GDN_HEADS
    p2 = p1 + GDN_HEADS
    p3 = p2 + V_A
    p4 = p3 + 2 * CONV_CH
    p5 = p4 + XA_W
    w_qkv = w_in[:, :p0].astype(BF16)
    w_ab = jnp.pad(w_in[:, p0:p2], ((0, 0), (0, LANES - 2 * GDN_HEADS))).astype(BF16)
    w_z = w_in[:, p2:p3].astype(BF16)
    w_glu = w_in[:, p3:p4].astype(BF16)
    w_qc = w_in[:, p4:p5].astype(BF16)
    w_gate = w_in[:, p5:].astype(BF16)
    pad_row = lambda v: jnp.pad(v, (0, LANES - v.shape[0])).reshape(1, LANES)
    row = lambda v: v.reshape(1, -1)
    consts = [
        row(norm_mix), w_qkv, w_ab, w_z, w_glu, w_qc, w_gate,
        gdn_conv_w, pad_row(gdn_a_log), pad_row(gdn_dt_bias), row(gdn_norm), w_gdn_out.astype(BF16),
        row(cc_glu_b), cc_dw_w, row(cc_dw_b), row(cc_ln_w), row(cc_ln_b), w_cc_out.astype(BF16),
        w_xa_out.astype(BF16), row(gate_b), w_o.astype(BF16), _level_masks(tm),
    ]
    mn, kvw = kvm.shape[1], kvm.shape[2]
    return pl.pallas_call(
        _mixer_kernel,
        grid=(bn, sn // tm),
        in_specs=[pl.BlockSpec((1, tm, d), lambda b, s: (b, s, 0)),
                  pl.BlockSpec((1, mn, kvw), lambda b, s: (b, 0, 0))]
                 + [_const_spec(c.shape) for c in consts],
        out_specs=pl.BlockSpec((1, tm, d), lambda b, s: (b, s, 0)),
        out_shape=jax.ShapeDtypeStruct(x.shape, x.dtype),
        scratch_shapes=[
            pltpu.VMEM((tm, d), BF16),
            pltpu.VMEM((QKV_HALO + tm, QKV_A), F32),
            pltpu.VMEM((tm, QK_A), F32),
            pltpu.VMEM((tm, QK_A), F32),
            pltpu.VMEM((tm, V_A), F32),
            pltpu.VMEM((tm, LANES), F32),
            pltpu.VMEM((tm, LANES), F32),
            pltpu.VMEM((tm, V_A), F32),
            pltpu.VMEM((CC_HALO + tm, CONV_CH), F32),
            pltpu.VMEM((tm, CONV_CH), F32),
            pltpu.VMEM((GDN_HEADS, GDN_DK, GDN_DV), F32),
            pltpu.VMEM((tm, d), F32),
        ],
        compiler_params=pltpu.CompilerParams(
            dimension_semantics=("arbitrary", "arbitrary"),
            vmem_limit_bytes=VMEM_LIMIT_BYTES),
        name="mixer",
    )(x, kvm, *consts)


def _ffn_kernel(x_ref, nffn_ref, wup_ref, dww_ref, dwb_ref, wdown_ref, nfin_ref, out_ref,
                gbuf_s, act_s, *, final):
    tm = x_ref.shape[1]
    f = dww_ref.shape[1]

    @pl.when(pl.program_id(1) == 0)
    def _():
        gbuf_s[0:FFN_HALO, :] = jnp.zeros((FFN_HALO, f), F32)

    x = x_ref[0]
    hb = _rmsnorm(x, nffn_ref[...]).astype(BF16)
    gbuf_s[FFN_HALO:FFN_HALO + tm, :] = _mm(hb, wup_ref[:, :f])
    act_s[...] = _mm(hb, wup_ref[:, f:])

    def emit(r0, c0, blk):
        rows, cols = slice(r0, r0 + CONV_ROWS), slice(c0, c0 + LANES)
        act_s[rows, cols] = _silu(blk + dwb_ref[:, cols]) * act_s[rows, cols]

    _causal_dwconv(gbuf_s, dww_ref, FFN_HALO, tm, FFN_CONV, emit)
    gbuf_s[0:FFN_HALO, :] = gbuf_s[tm:tm + FFN_HALO, :]
    y = x + _mm(act_s[...], wdown_ref[...])
    if final:
        y = _rmsnorm(y, nfin_ref[...])
    out_ref[0] = y


def _ffn(x, norm_ffn, w_up, ffn_dw_w, ffn_dw_b, w_down, norm_final, final):
    bn, sn, d = x.shape
    tm = ROW_TILE
    f = w_down.shape[0]
    consts = [norm_ffn.reshape(1, d), w_up.astype(BF16), ffn_dw_w, ffn_dw_b.reshape(1, f),
              w_down.astype(BF16), norm_final.reshape(1, d)]
    return pl.pallas_call(
        functools.partial(_ffn_kernel, final=final),
        grid=(bn, sn // tm),
        in_specs=[pl.BlockSpec((1, tm, d), lambda b, s: (b, s, 0))]
                 + [_const_spec(c.shape) for c in consts],
        out_specs=pl.BlockSpec((1, tm, d), lambda b, s: (b, s, 0)),
        out_shape=jax.ShapeDtypeStruct(x.shape, x.dtype),
        scratch_shapes=[
            pltpu.VMEM((FFN_HALO + tm, f), F32),
            pltpu.VMEM((tm, f), F32),
        ],
        compiler_params=pltpu.CompilerParams(
            dimension_semantics=("arbitrary", "arbitrary"),
            vmem_limit_bytes=VMEM_LIMIT_BYTES),
        name="ffn",
    )(x, *consts)


def kernel(x, mem, norm_mix, w_in, gdn_conv_w, gdn_dt_bias, gdn_a_log, gdn_norm, w_gdn_out, cc_glu_b, cc_dw_w, cc_dw_b, cc_ln_w, cc_ln_b, w_cc_out, mem_norm, w_mem_kv, w_xa_out, gate_b, w_o, norm_ffn, w_up, ffn_dw_w, ffn_dw_b, w_down, norm_final):
    n_layers = w_in.shape[0]
    assert x.shape[1] % ROW_TILE == 0 and ROW_TILE % GDN_CHUNK == 0 and ROW_TILE % CONV_ROWS == 0
    kvm = _memkv(mem, mem_norm, w_mem_kv)
    for l in range(n_layers):
        x = _mixer(x, kvm[l], norm_mix[l], w_in[l], gdn_conv_w[l], gdn_dt_bias[l], gdn_a_log[l],
                   gdn_norm[l], w_gdn_out[l], cc_glu_b[l], cc_dw_w[l], cc_dw_b[l], cc_ln_w[l],
                   cc_ln_b[l], w_cc_out[l], w_xa_out[l], gate_b[l], w_o[l])
        x = _ffn(x, norm_ffn[l], w_up[l], ffn_dw_w[l], ffn_dw_b[l], w_down[l], norm_final,
                 final=(l == n_layers - 1))
    return x
```

```python
import functools

import numpy as np
import jax
import jax.numpy as jnp
from jax import lax
from jax.experimental import pallas as pl
from jax.experimental.pallas import tpu as pltpu

EPS = 1e-6
GDN_HEADS = 4
GDN_DK = 128
GDN_DV = 128
GDN_CONV = 4
GDN_CHUNK = 64
QK_A = GDN_HEADS * GDN_DK
V_A = GDN_HEADS * GDN_DV
QKV_A = 2 * QK_A + V_A
CONV_CH = 512
CONV_K = 31
XA_HEADS = 4
XA_DH = 128
XA_W = XA_HEADS * XA_DH
N_BRANCH = 3
FFN_CONV = 3

LANES = 128
SUBLANES = 8
ROW_TILE = 256
CONV_ROWS = 64
GATE_COLS = 512
FILL_RATIO = 2
QKV_HALO = SUBLANES
CC_HALO = 4 * SUBLANES
FFN_HALO = SUBLANES
VMEM_LIMIT_BYTES = 56 * 1024 * 1024

BF16 = jnp.bfloat16
F32 = jnp.float32


def _mm(a, b):
    return jnp.dot(a.astype(BF16), b.astype(BF16), preferred_element_type=F32)


def _mm_nt(a, b):
    return lax.dot_general(a.astype(BF16), b.astype(BF16), (((1,), (1,)), ((), ())),
                           preferred_element_type=F32)


def _mm_tn(a, b):
    return lax.dot_general(a.astype(BF16), b.astype(BF16), (((0,), (0,)), ((), ())),
                           preferred_element_type=F32)


def _rmsnorm(x, w_row):
    ms = jnp.mean(x * x, axis=-1, keepdims=True)
    return x * lax.rsqrt(ms + EPS) * w_row


def _sigmoid(x):
    return 0.5 * jnp.tanh(0.5 * x) + 0.5


def _silu(x):
    return x * _sigmoid(x)


def _softplus(x):
    return jnp.maximum(x, 0.0) + jnp.log(1.0 + jnp.exp(-jnp.abs(x)))


def _interleave(main, fillers, ratio):
    live = list(fillers)

    def fill(n):
        while n > 0 and live:
            for g in list(live):
                if n == 0:
                    break
                try:
                    next(g)
                    n -= 1
                except StopIteration:
                    live.remove(g)

    for _ in main:
        fill(ratio)
    fill(float("inf"))


def _to_slabs(buf_ref, halo, val):
    for cb in range(buf_ref.shape[0]):
        buf_ref[cb, halo:halo + val.shape[0], :] = val[:, cb * LANES:(cb + 1) * LANES]


def _causal_dwconv(buf_ref, w_ref, halo, n_rows, n_taps, emit):
    for cb in range(buf_ref.shape[0]):
        c0 = cb * LANES
        for r0 in range(0, n_rows, CONV_ROWS):
            acc = None
            for k in range(n_taps):
                start = halo - (n_taps - 1) + k + r0
                term = w_ref[k:k + 1, c0:c0 + LANES] * buf_ref[cb, start:start + CONV_ROWS, :]
                acc = term if acc is None else acc + term
            emit(r0, c0, acc)
            yield
    buf_ref[:, 0:halo, :] = buf_ref[:, n_rows:n_rows + halo, :]


def _memkv_kernel(mem_ref, nrm_ref, w_ref, out_ref):
    m = _rmsnorm(mem_ref[0], nrm_ref[0])
    out_ref[0, 0] = _mm(m, w_ref[0]).astype(out_ref.dtype)


def _memkv(mem, mem_norm, w_mem_kv):
    n_layers = w_mem_kv.shape[0]
    bn, mn, d = mem.shape
    n_out = w_mem_kv.shape[2]
    return pl.pallas_call(
        _memkv_kernel,
        grid=(n_layers, bn),
        in_specs=[
            pl.BlockSpec((1, mn, d), lambda l, b: (b, 0, 0)),
            pl.BlockSpec((1, 1, d), lambda l, b: (l, 0, 0)),
            pl.BlockSpec((1, d, n_out), lambda l, b: (l, 0, 0)),
        ],
        out_specs=pl.BlockSpec((1, 1, mn, n_out), lambda l, b: (l, b, 0, 0)),
        out_shape=jax.ShapeDtypeStruct((n_layers, bn, mn, n_out), BF16),
        compiler_params=pltpu.CompilerParams(
            dimension_semantics=("arbitrary", "arbitrary"),
            vmem_limit_bytes=VMEM_LIMIT_BYTES),
        name="mem_kv",
    )(mem, mem_norm.reshape(n_layers, 1, d), w_mem_kv.astype(BF16))


def _mixer_kernel(x_ref, kvm_ref, nmix_ref, wqkv_ref, wab_ref, wz_ref, wglu_ref, wqc_ref,
                  wgate_ref, convw_ref, alog_ref, dtb_ref, gnorm_ref, wgdn_ref,
                  glub_ref, dww_ref, dwb_ref, lnw_ref, lnb_ref, wcc_ref,
                  wxa_ref, gateb_ref, wo_ref, lvl_ref,
                  out_ref,
                  h_s, qkv_s, q_s, k_s, v_s, o_s, ubuf_s, conv_s, att_s, state_s, gate_s, y_s):
    tm = x_ref.shape[1]
    d = x_ref.shape[2]

    @pl.when(pl.program_id(1) == 0)
    def _():
        qkv_s[:, 0:QKV_HALO, :] = jnp.zeros((qkv_s.shape[0], QKV_HALO, LANES), F32)
        ubuf_s[:, 0:CC_HALO, :] = jnp.zeros((ubuf_s.shape[0], CC_HALO, LANES), F32)
        state_s[...] = jnp.zeros(state_s.shape, F32)

    x = x_ref[0]
    h_s[...] = _rmsnorm(x, nmix_ref[...]).astype(BF16)

    gate_cols_written = set()

    def gates():
        for c0 in range(0, N_BRANCH * d, GATE_COLS):
            cols = slice(c0, c0 + GATE_COLS)
            gate_s[:, cols] = _sigmoid(_mm(h_s[...], wgate_ref[:, cols]) + gateb_ref[:, cols])
            gate_cols_written.add(c0)
            yield

    def gated(branch, y):
        assert all(c0 in gate_cols_written for c0 in range(branch * d, (branch + 1) * d, GATE_COLS))
        y_s[branch] = gate_s[:, branch * d:(branch + 1) * d] * y

    def branch_a():
        _to_slabs(qkv_s, QKV_HALO, _mm(h_s[...], wqkv_ref[...]))
        yield

        def emit_qkv(r0, c0, blk):
            blk = _silu(blk)
            rows = slice(r0, r0 + CONV_ROWS)
            if c0 < 2 * QK_A:
                blk = blk * lax.rsqrt(jnp.sum(blk * blk, axis=-1, keepdims=True) + EPS)
            if c0 < QK_A:
                q_s[rows, c0:c0 + LANES] = blk * (GDN_DK ** -0.5)
            elif c0 < 2 * QK_A:
                k_s[rows, c0 - QK_A:c0 - QK_A + LANES] = blk
            else:
                v_s[rows, c0 - 2 * QK_A:c0 - 2 * QK_A + LANES] = blk

        for _ in _causal_dwconv(qkv_s, convw_ref, QKV_HALO, tm, GDN_CONV, emit_qkv):
            pass
        yield

        ab = _mm(h_s[...], wab_ref[...])
        g = -jnp.exp(alog_ref[...]) * _softplus(ab + dtb_ref[...])
        beta = _sigmoid(ab)

        n_chunks = tm // GDN_CHUNK
        chunk_bits = GDN_CHUNK.bit_length() - 1
        ri = lax.broadcasted_iota(jnp.int32, (tm, tm), 0)
        ci = lax.broadcasted_iota(jnp.int32, (tm, tm), 1)
        same_chunk = (ri >> chunk_bits) == (ci >> chunk_bits)
        causal = same_chunk & (ri >= ci)
        strict = same_chunk & (ri > ci)
        eye = (ri == ci).astype(F32)

        tri = causal.astype(BF16)
        g_hi = g.astype(BF16)
        g_r1 = g - g_hi.astype(F32)
        g_mid = g_r1.astype(BF16)
        g_lo = (g_r1 - g_mid.astype(F32)).astype(BF16)
        gc = _mm(tri, g_hi) + _mm(tri, g_mid) + _mm(tri, g_lo)
        gct = gc.T
        eg = jnp.exp(gc)
        egl_parts, eg_last = [], []
        for c in range(n_chunks):
            gl = gc[(c + 1) * GDN_CHUNK - 1:(c + 1) * GDN_CHUNK, :]
            egl_parts.append(jnp.exp(gl - gc[c * GDN_CHUNK:(c + 1) * GDN_CHUNK, :]))
            eg_last.append(jnp.exp(gl))
        egl = jnp.concatenate(egl_parts, axis=0)
        yield

        heads = range(GDN_HEADS)
        lanes = [slice(hh * GDN_DK, (hh + 1) * GDN_DK) for hh in heads]
        q_h = [q_s[:, ls] for ls in lanes]
        k_h = [k_s[:, ls] for ls in lanes]
        bcol = [beta[:, GDN_HEADS + hh:GDN_HEADS + hh + 1] for hh in heads]
        egcol = [eg[:, hh:hh + 1] for hh in heads]
        kb = [k_h[hh] * bcol[hh] for hh in heads]
        decay = [jnp.exp(jnp.where(causal, gc[:, hh:hh + 1] - gct[hh:hh + 1, :], -jnp.inf))
                 for hh in heads]
        a_mat = [jnp.where(strict, _mm_nt(kb[hh], k_h[hh]) * decay[hh], 0.0) for hh in heads]
        yield
        qk = [jnp.where(causal, _mm_nt(q_h[hh], k_h[hh]) * decay[hh], 0.0) for hh in heads]
        t_mat = [eye - a_mat[hh] * lvl_ref[0] for hh in heads]
        yield
        for j in range(1, chunk_bits):
            xt = [_mm(a_mat[hh] * lvl_ref[j], t_mat[hh]) for hh in heads]
            yield
            t_mat = [t_mat[hh] - _mm(t_mat[hh], xt[hh]) for hh in heads]
            yield
        uw = [_mm(t_mat[hh], jnp.concatenate([v_s[:, lanes[hh]] * bcol[hh], kb[hh] * egcol[hh]], axis=1))
              for hh in heads]
        yield
        loc = [_mm(qk[hh], uw[hh]) for hh in heads]
        k_dec = [k_h[hh] * egl[:, hh:hh + 1] for hh in heads]
        chunk_rows = [slice(c * GDN_CHUNK, (c + 1) * GDN_CHUNK) for c in range(n_chunks)]
        kuw = [[_mm_tn(k_dec[hh][rows], uw[hh][rows]) for rows in chunk_rows] for hh in heads]
        yield
        q_til = [q_h[hh] * egcol[hh] - loc[hh][:, GDN_DV:] for hh in heads]
        st = [state_s[hh] for hh in heads]
        o_parts = [[] for _ in heads]
        for c, rows in enumerate(chunk_rows):
            for hh in heads:
                r = _mm(jnp.concatenate([kuw[hh][c][:, GDN_DV:], q_til[hh][rows]], axis=0), st[hh])
                o_parts[hh].append(r[GDN_DK:])
                st[hh] = st[hh] * eg_last[c][:, hh:hh + 1] + kuw[hh][c][:, :GDN_DV] - r[:GDN_DK]
            yield
        z = _mm(h_s[...], wz_ref[...])
        for hh in heads:
            state_s[hh] = st[hh]
            o = jnp.concatenate(o_parts[hh], axis=0) + loc[hh][:, :GDN_DV]
            o_s[:, lanes[hh]] = _rmsnorm(o, gnorm_ref[...]) * _silu(z[:, lanes[hh]])
        yield
        gated(0, _mm(o_s[...], wgdn_ref[...]))

    def branch_b():
        glu = _mm(h_s[...], wglu_ref[...]) + glub_ref[...]
        _to_slabs(ubuf_s, CC_HALO, glu[:, :CONV_CH] * _sigmoid(glu[:, CONV_CH:]))
        yield

        def emit_cc(r0, c0, blk):
            conv_s[r0:r0 + CONV_ROWS, c0:c0 + LANES] = blk + dwb_ref[:, c0:c0 + LANES]

        yield from _causal_dwconv(ubuf_s, dww_ref, CC_HALO, tm, CONV_K, emit_cc)
        for r0 in range(0, tm, CONV_ROWS):
            blk = conv_s[r0:r0 + CONV_ROWS, :]
            mu = jnp.mean(blk, axis=-1, keepdims=True)
            cen = blk - mu
            var = jnp.mean(cen * cen, axis=-1, keepdims=True)
            conv_s[r0:r0 + CONV_ROWS, :] = _silu(cen * lax.rsqrt(var + EPS) * lnw_ref[...] + lnb_ref[...])
            yield
        gated(1, _mm(conv_s[...], wcc_ref[...]))

    def branch_c():
        qc = _mm(h_s[...], wqc_ref[...])
        yield
        for hh in range(XA_HEADS):
            ls = slice(hh * XA_DH, (hh + 1) * XA_DH)
            k_m = kvm_ref[0, :, hh * XA_DH:(hh + 1) * XA_DH]
            v_m = kvm_ref[0, :, XA_W + hh * XA_DH:XA_W + (hh + 1) * XA_DH]
            sc = _mm_nt(qc[:, ls], k_m) * (XA_DH ** -0.5)
            ex = jnp.exp(sc - jnp.max(sc, axis=-1, keepdims=True))
            att_s[:, ls] = _mm(ex, v_m) * (1.0 / jnp.sum(ex, axis=-1, keepdims=True))
            yield
        gated(2, _mm(att_s[...], wxa_ref[...]))

    _interleave(branch_a(), [gates(), branch_b(), branch_c()], FILL_RATIO)
    merged = y_s[0] + y_s[1] + y_s[2]
    out_ref[0] = x + _mm(merged, wo_ref[...])


def _level_masks(tm):
    r = np.arange(tm)[:, None]
    c = np.arange(tm)[None, :]
    same_chunk = (r // GDN_CHUNK) == (c // GDN_CHUNK)
    n_levels = GDN_CHUNK.bit_length() - 1
    masks = [same_chunk & (r > c) & (((r ^ c) >> j) == 1) for j in range(n_levels)]
    return jnp.asarray(np.stack(masks).astype(np.float32))


def _const_spec(shape):
    nd = len(shape)
    return pl.BlockSpec(shape, lambda b, s: (0,) * nd, pipeline_mode=pl.Buffered(1))


def _mixer(x, kvm, norm_mix, w_in, gdn_conv_w, gdn_dt_bias, gdn_a_log, gdn_norm, w_gdn_out,
           cc_glu_b, cc_dw_w, cc_dw_b, cc_ln_w, cc_ln_b, w_cc_out, w_xa_out, gate_b, w_o):
    bn, sn, d = x.shape
    tm = ROW_TILE
    p0 = QKV_A
    p1 = p0 + GDN_HEADS
    p2 = p1 + GDN_HEADS
    p3 = p2 + V_A
    p4 = p3 + 2 * CONV_CH
    p5 = p4 + XA_W
    w_qkv = w_in[:, :p0].astype(BF16)
    w_ab = jnp.pad(w_in[:, p0:p2], ((0, 0), (0, LANES - 2 * GDN_HEADS))).astype(BF16)
    w_z = w_in[:, p2:p3].astype(BF16)
    w_glu = w_in[:, p3:p4].astype(BF16)
    w_qc = w_in[:, p4:p5].astype(BF16)
    w_gate = w_in[:, p5:].astype(BF16)
    pad_row = lambda v: jnp.pad(v, (0, LANES - v.shape[0])).reshape(1, LANES)
    row = lambda v: v.reshape(1, -1)
    consts = [
        row(norm_mix), w_qkv, w_ab, w_z, w_glu, w_qc, w_gate,
        gdn_conv_w, pad_row(gdn_a_log), pad_row(gdn_dt_bias), row(gdn_norm), w_gdn_out.astype(BF16),
        row(cc_glu_b), cc_dw_w, row(cc_dw_b), row(cc_ln_w), row(cc_ln_b), w_cc_out.astype(BF16),
        w_xa_out.astype(BF16), row(gate_b), w_o.astype(BF16), _level_masks(tm),
    ]
    mn, kvw = kvm.shape[1], kvm.shape[2]
    return pl.pallas_call(
        _mixer_kernel,
        grid=(bn, sn // tm),
        in_specs=[pl.BlockSpec((1, tm, d), lambda b, s: (b, s, 0)),
                  pl.BlockSpec((1, mn, kvw), lambda b, s: (b, 0, 0))]
                 + [_const_spec(c.shape) for c in consts],
        out_specs=pl.BlockSpec((1, tm, d), lambda b, s: (b, s, 0)),
        out_shape=jax.ShapeDtypeStruct(x.shape, x.dtype),
        scratch_shapes=[
            pltpu.VMEM((tm, d), BF16),
            pltpu.VMEM((QKV_A // LANES, QKV_HALO + tm, LANES), F32),
            pltpu.VMEM((tm, QK_A), F32),
            pltpu.VMEM((tm, QK_A), F32),
            pltpu.VMEM((tm, V_A), F32),
            pltpu.VMEM((tm, V_A), F32),
            pltpu.VMEM((CONV_CH // LANES, CC_HALO + tm, LANES), F32),
            pltpu.VMEM((tm, CONV_CH), F32),
            pltpu.VMEM((tm, XA_W), F32),
            pltpu.VMEM((GDN_HEADS, GDN_DK, GDN_DV), F32),
            pltpu.VMEM((tm, N_BRANCH * d), F32),
            pltpu.VMEM((N_BRANCH, tm, d), F32),
        ],
        compiler_params=pltpu.CompilerParams(
            dimension_semantics=("arbitrary", "arbitrary"),
            vmem_limit_bytes=VMEM_LIMIT_BYTES),
        name="mixer",
    )(x, kvm, *consts)


def _ffn_kernel(x_ref, nffn_ref, wup_ref, dww_ref, dwb_ref, wdown_ref, nfin_ref, out_ref,
                gbuf_s, act_s, *, final):
    tm = x_ref.shape[1]
    f = dww_ref.shape[1]

    @pl.when(pl.program_id(1) == 0)
    def _():
        gbuf_s[:, 0:FFN_HALO, :] = jnp.zeros((gbuf_s.shape[0], FFN_HALO, LANES), F32)

    x = x_ref[0]
    hb = _rmsnorm(x, nffn_ref[...]).astype(BF16)
    _to_slabs(gbuf_s, FFN_HALO, _mm(hb, wup_ref[:, :f]))
    act_s[...] = _mm(hb, wup_ref[:, f:])

    def emit(r0, c0, blk):
        rows, cols = slice(r0, r0 + CONV_ROWS), slice(c0, c0 + LANES)
        act_s[rows, cols] = _silu(blk + dwb_ref[:, cols]) * act_s[rows, cols]

    for _ in _causal_dwconv(gbuf_s, dww_ref, FFN_HALO, tm, FFN_CONV, emit):
        pass
    y = x + _mm(act_s[...], wdown_ref[...])
    if final:
        y = _rmsnorm(y, nfin_ref[...])
    out_ref[0] = y


def _ffn(x, norm_ffn, w_up, ffn_dw_w, ffn_dw_b, w_down, norm_final, final):
    bn, sn, d = x.shape
    tm = ROW_TILE
    f = w_down.shape[0]
    consts = [norm_ffn.reshape(1, d), w_up.astype(BF16), ffn_dw_w, ffn_dw_b.reshape(1, f),
              w_down.astype(BF16), norm_final.reshape(1, d)]
    return pl.pallas_call(
        functools.partial(_ffn_kernel, final=final),
        grid=(bn, sn // tm),
        in_specs=[pl.BlockSpec((1, tm, d), lambda b, s: (b, s, 0))]
                 + [_const_spec(c.shape) for c in consts],
        out_specs=pl.BlockSpec((1, tm, d), lambda b, s: (b, s, 0)),
        out_shape=jax.ShapeDtypeStruct(x.shape, x.dtype),
        scratch_shapes=[
            pltpu.VMEM((f // LANES, FFN_HALO + tm, LANES), F32),
            pltpu.VMEM((tm, f), F32),
        ],
        compiler_params=pltpu.CompilerParams(
            dimension_semantics=("arbitrary", "arbitrary"),
            vmem_limit_bytes=VMEM_LIMIT_BYTES),
        name="ffn",
    )(x, *consts)


def kernel(x, mem, norm_mix, w_in, gdn_conv_w, gdn_dt_bias, gdn_a_log, gdn_norm, w_gdn_out, cc_glu_b, cc_dw_w, cc_dw_b, cc_ln_w, cc_ln_b, w_cc_out, mem_norm, w_mem_kv, w_xa_out, gate_b, w_o, norm_ffn, w_up, ffn_dw_w, ffn_dw_b, w_down, norm_final):
    n_layers = w_in.shape[0]
    assert x.shape[1] % ROW_TILE == 0 and ROW_TILE % GDN_CHUNK == 0 and ROW_TILE % CONV_ROWS == 0
    kvm = _memkv(mem, mem_norm, w_mem_kv)
    for l in range(n_layers):
        x = _mixer(x, kvm[l], norm_mix[l], w_in[l], gdn_conv_w[l], gdn_dt_bias[l], gdn_a_log[l],
                   gdn_norm[l], w_gdn_out[l], cc_glu_b[l], cc_dw_w[l], cc_dw_b[l], cc_ln_w[l],
                   cc_ln_b[l], w_cc_out[l], w_xa_out[l], gate_b[l], w_o[l])
        x = _ffn(x, norm_ffn[l], w_up[l], ffn_dw_w[l], ffn_dw_b[l], w_down[l], norm_final,
                 final=(l == n_layers - 1))
    return x
```

```python
import functools

import numpy as np
import jax
import jax.numpy as jnp
from jax import lax
from jax.experimental import pallas as pl
from jax.experimental.pallas import tpu as pltpu

EPS = 1e-6
GDN_HEADS = 4
GDN_DK = 128
GDN_DV = 128
GDN_CONV = 4
GDN_CHUNK = 64
QK_A = GDN_HEADS * GDN_DK
V_A = GDN_HEADS * GDN_DV
QKV_A = 2 * QK_A + V_A
CONV_CH = 512
CONV_K = 31
XA_HEADS = 4
XA_DH = 128
XA_W = XA_HEADS * XA_DH
N_BRANCH = 3
FFN_CONV = 3

LANES = 128
SUBLANES = 8
MIXER_ROWS = 256
FFN_ROWS = 512
CONV_ROWS = 64
GATE_COLS = 512
FILL_RATIO = 2
QKV_HALO = SUBLANES
CC_HALO = 4 * SUBLANES
FFN_HALO = SUBLANES
VMEM_LIMIT_BYTES = 56 * 1024 * 1024

BF16 = jnp.bfloat16
F32 = jnp.float32


def _mm(a, b):
    return jnp.dot(a.astype(BF16), b.astype(BF16), preferred_element_type=F32)


def _mm_nt(a, b):
    return lax.dot_general(a.astype(BF16), b.astype(BF16), (((1,), (1,)), ((), ())),
                           preferred_element_type=F32)


def _mm_tn(a, b):
    return lax.dot_general(a.astype(BF16), b.astype(BF16), (((0,), (0,)), ((), ())),
                           preferred_element_type=F32)


def _rmsnorm(x, w_row):
    ms = jnp.mean(x * x, axis=-1, keepdims=True)
    return x * lax.rsqrt(ms + EPS) * w_row


def _sigmoid(x):
    return 0.5 * jnp.tanh(0.5 * x) + 0.5


def _silu(x):
    return x * _sigmoid(x)


def _softplus(x):
    return jnp.maximum(x, 0.0) + jnp.log(1.0 + jnp.exp(-jnp.abs(x)))


def _interleave(main, fillers, ratio):
    live = list(fillers)

    def fill(n):
        while n > 0 and live:
            for g in list(live):
                if n == 0:
                    break
                try:
                    next(g)
                    n -= 1
                except StopIteration:
                    live.remove(g)

    for _ in main:
        fill(ratio)
    fill(float("inf"))


def _to_slabs(buf_ref, halo, val):
    for cb in range(buf_ref.shape[0]):
        buf_ref[cb, halo:halo + val.shape[0], :] = val[:, cb * LANES:(cb + 1) * LANES]


def _causal_dwconv(buf_ref, w_ref, halo, n_rows, n_taps, emit):
    for cb in range(buf_ref.shape[0]):
        c0 = cb * LANES
        for r0 in range(0, n_rows, CONV_ROWS):
            acc = None
            for k in range(n_taps):
                start = halo - (n_taps - 1) + k + r0
                term = w_ref[k:k + 1, c0:c0 + LANES] * buf_ref[cb, start:start + CONV_ROWS, :]
                acc = term if acc is None else acc + term
            emit(r0, c0, acc)
            yield
    buf_ref[:, 0:halo, :] = buf_ref[:, n_rows:n_rows + halo, :]


def _memkv_kernel(mem_ref, nrm_ref, w_ref, out_ref):
    m = _rmsnorm(mem_ref[0], nrm_ref[0])
    out_ref[0, 0] = _mm(m, w_ref[0]).astype(out_ref.dtype)


def _memkv(mem, mem_norm, w_mem_kv):
    n_layers = w_mem_kv.shape[0]
    bn, mn, d = mem.shape
    n_out = w_mem_kv.shape[2]
    return pl.pallas_call(
        _memkv_kernel,
        grid=(n_layers, bn),
        in_specs=[
            pl.BlockSpec((1, mn, d), lambda l, b: (b, 0, 0)),
            pl.BlockSpec((1, 1, d), lambda l, b: (l, 0, 0)),
            pl.BlockSpec((1, d, n_out), lambda l, b: (l, 0, 0)),
        ],
        out_specs=pl.BlockSpec((1, 1, mn, n_out), lambda l, b: (l, b, 0, 0)),
        out_shape=jax.ShapeDtypeStruct((n_layers, bn, mn, n_out), BF16),
        compiler_params=pltpu.CompilerParams(
            dimension_semantics=("arbitrary", "arbitrary"),
            vmem_limit_bytes=VMEM_LIMIT_BYTES),
        name="mem_kv",
    )(mem, mem_norm.reshape(n_layers, 1, d), w_mem_kv.astype(BF16))


def _mixer_kernel(x_ref, kvm_ref, nmix_ref, wqkv_ref, wab_ref, wz_ref, wglu_ref, wqc_ref,
                  wgate_ref, convw_ref, alog_ref, dtb_ref, gnorm_ref, wgdn_ref,
                  glub_ref, dww_ref, dwb_ref, lnw_ref, lnb_ref, wcc_ref,
                  wxa_ref, gateb_ref, wo_ref, lvl_ref,
                  out_ref,
                  h_s, qkv_s, q_s, k_s, v_s, o_s, ubuf_s, conv_s, att_s, state_s, gate_s, y_s):
    tm = x_ref.shape[1]
    d = x_ref.shape[2]

    @pl.when(pl.program_id(1) == 0)
    def _():
        qkv_s[:, 0:QKV_HALO, :] = jnp.zeros((qkv_s.shape[0], QKV_HALO, LANES), F32)
        ubuf_s[:, 0:CC_HALO, :] = jnp.zeros((ubuf_s.shape[0], CC_HALO, LANES), F32)
        state_s[...] = jnp.zeros(state_s.shape, F32)

    x = x_ref[0]
    h_s[...] = _rmsnorm(x, nmix_ref[...]).astype(BF16)

    gate_cols_written = set()

    def gates():
        for c0 in range(0, N_BRANCH * d, GATE_COLS):
            cols = slice(c0, c0 + GATE_COLS)
            gate_s[:, cols] = _sigmoid(_mm(h_s[...], wgate_ref[:, cols]) + gateb_ref[:, cols])
            gate_cols_written.add(c0)
            yield

    def gated(branch, y):
        assert all(c0 in gate_cols_written for c0 in range(branch * d, (branch + 1) * d, GATE_COLS))
        y_s[branch] = gate_s[:, branch * d:(branch + 1) * d] * y

    def branch_a():
        _to_slabs(qkv_s, QKV_HALO, _mm(h_s[...], wqkv_ref[...]))
        yield

        def emit_qkv(r0, c0, blk):
            blk = _silu(blk)
            rows = slice(r0, r0 + CONV_ROWS)
            if c0 < 2 * QK_A:
                blk = blk * lax.rsqrt(jnp.sum(blk * blk, axis=-1, keepdims=True) + EPS)
            if c0 < QK_A:
                q_s[rows, c0:c0 + LANES] = blk * (GDN_DK ** -0.5)
            elif c0 < 2 * QK_A:
                k_s[rows, c0 - QK_A:c0 - QK_A + LANES] = blk
            else:
                v_s[rows, c0 - 2 * QK_A:c0 - 2 * QK_A + LANES] = blk

        for _ in _causal_dwconv(qkv_s, convw_ref, QKV_HALO, tm, GDN_CONV, emit_qkv):
            pass
        yield

        ab = _mm(h_s[...], wab_ref[...])
        g = -jnp.exp(alog_ref[...]) * _softplus(ab + dtb_ref[...])
        beta = _sigmoid(ab)

        n_chunks = tm // GDN_CHUNK
        chunk_bits = GDN_CHUNK.bit_length() - 1
        ri = lax.broadcasted_iota(jnp.int32, (tm, tm), 0)
        ci = lax.broadcasted_iota(jnp.int32, (tm, tm), 1)
        same_chunk = (ri >> chunk_bits) == (ci >> chunk_bits)
        causal = same_chunk & (ri >= ci)
        strict = same_chunk & (ri > ci)
        eye = (ri == ci).astype(F32)

        tri = causal.astype(BF16)
        g_hi = g.astype(BF16)
        g_r1 = g - g_hi.astype(F32)
        g_mid = g_r1.astype(BF16)
        g_lo = (g_r1 - g_mid.astype(F32)).astype(BF16)
        gc = _mm(tri, g_hi) + _mm(tri, g_mid) + _mm(tri, g_lo)
        gct = gc.T
        eg = jnp.exp(gc)
        egl_parts, eg_last = [], []
        for c in range(n_chunks):
            gl = gc[(c + 1) * GDN_CHUNK - 1:(c + 1) * GDN_CHUNK, :]
            egl_parts.append(jnp.exp(gl - gc[c * GDN_CHUNK:(c + 1) * GDN_CHUNK, :]))
            eg_last.append(jnp.exp(gl))
        egl = jnp.concatenate(egl_parts, axis=0)
        yield

        heads = range(GDN_HEADS)
        lanes = [slice(hh * GDN_DK, (hh + 1) * GDN_DK) for hh in heads]
        q_h = [q_s[:, ls] for ls in lanes]
        k_h = [k_s[:, ls] for ls in lanes]
        bcol = [beta[:, GDN_HEADS + hh:GDN_HEADS + hh + 1] for hh in heads]
        egcol = [eg[:, hh:hh + 1] for hh in heads]
        kb = [k_h[hh] * bcol[hh] for hh in heads]
        decay = [jnp.exp(jnp.where(causal, gc[:, hh:hh + 1] - gct[hh:hh + 1, :], -jnp.inf))
                 for hh in heads]
        a_mat = [jnp.where(strict, _mm_nt(kb[hh], k_h[hh]) * decay[hh], 0.0) for hh in heads]
        yield
        qk = [jnp.where(causal, _mm_nt(q_h[hh], k_h[hh]) * decay[hh], 0.0) for hh in heads]
        t_mat = [eye - a_mat[hh] * lvl_ref[0] for hh in heads]
        yield
        for j in range(1, chunk_bits):
            xt = [_mm(a_mat[hh] * lvl_ref[j], t_mat[hh]) for hh in heads]
            yield
            t_mat = [t_mat[hh] - _mm(t_mat[hh], xt[hh]) for hh in heads]
            yield
        uw = [_mm(t_mat[hh], jnp.concatenate([v_s[:, lanes[hh]] * bcol[hh], kb[hh] * egcol[hh]], axis=1))
              for hh in heads]
        yield
        loc = [_mm(qk[hh], uw[hh]) for hh in heads]
        k_dec = [k_h[hh] * egl[:, hh:hh + 1] for hh in heads]
        chunk_rows = [slice(c * GDN_CHUNK, (c + 1) * GDN_CHUNK) for c in range(n_chunks)]
        kuw = [[_mm_tn(k_dec[hh][rows], uw[hh][rows]) for rows in chunk_rows] for hh in heads]
        yield
        q_til = [q_h[hh] * egcol[hh] - loc[hh][:, GDN_DV:] for hh in heads]
        st = [state_s[hh] for hh in heads]
        o_parts = [[] for _ in heads]
        for c, rows in enumerate(chunk_rows):
            for hh in heads:
                r = _mm(jnp.concatenate([kuw[hh][c][:, GDN_DV:], q_til[hh][rows]], axis=0), st[hh])
                o_parts[hh].append(r[GDN_DK:])
                st[hh] = st[hh] * eg_last[c][:, hh:hh + 1] + kuw[hh][c][:, :GDN_DV] - r[:GDN_DK]
            yield
        z = _mm(h_s[...], wz_ref[...])
        for hh in heads:
            state_s[hh] = st[hh]
            o = jnp.concatenate(o_parts[hh], axis=0) + loc[hh][:, :GDN_DV]
            o_s[:, lanes[hh]] = _rmsnorm(o, gnorm_ref[...]) * _silu(z[:, lanes[hh]])
        yield
        gated(0, _mm(o_s[...], wgdn_ref[...]))

    def branch_b():
        glu = _mm(h_s[...], wglu_ref[...]) + glub_ref[...]
        _to_slabs(ubuf_s, CC_HALO, glu[:, :CONV_CH] * _sigmoid(glu[:, CONV_CH:]))
        yield

        def emit_cc(r0, c0, blk):
            conv_s[r0:r0 + CONV_ROWS, c0:c0 + LANES] = blk + dwb_ref[:, c0:c0 + LANES]

        yield from _causal_dwconv(ubuf_s, dww_ref, CC_HALO, tm, CONV_K, emit_cc)
        for r0 in range(0, tm, CONV_ROWS):
            blk = conv_s[r0:r0 + CONV_ROWS, :]
            mu = jnp.mean(blk, axis=-1, keepdims=True)
            cen = blk - mu
            var = jnp.mean(cen * cen, axis=-1, keepdims=True)
            conv_s[r0:r0 + CONV_ROWS, :] = _silu(cen * lax.rsqrt(var + EPS) * lnw_ref[...] + lnb_ref[...])
            yield
        gated(1, _mm(conv_s[...], wcc_ref[...]))

    def branch_c():
        qc = _mm(h_s[...], wqc_ref[...])
        yield
        for hh in range(XA_HEADS):
            ls = slice(hh * XA_DH, (hh + 1) * XA_DH)
            k_m = kvm_ref[0, :, hh * XA_DH:(hh + 1) * XA_DH]
            v_m = kvm_ref[0, :, XA_W + hh * XA_DH:XA_W + (hh + 1) * XA_DH]
            sc = _mm_nt(qc[:, ls], k_m) * (XA_DH ** -0.5)
            ex = jnp.exp(sc - jnp.max(sc, axis=-1, keepdims=True))
            att_s[:, ls] = _mm(ex, v_m) * (1.0 / jnp.sum(ex, axis=-1, keepdims=True))
            yield
        gated(2, _mm(att_s[...], wxa_ref[...]))

    _interleave(branch_a(), [gates(), branch_b(), branch_c()], FILL_RATIO)
    merged = y_s[0] + y_s[1] + y_s[2]
    out_ref[0] = x + _mm(merged, wo_ref[...])


def _level_masks(tm):
    r = np.arange(tm)[:, None]
    c = np.arange(tm)[None, :]
    same_chunk = (r // GDN_CHUNK) == (c // GDN_CHUNK)
    n_levels = GDN_CHUNK.bit_length() - 1
    masks = [same_chunk & (r > c) & (((r ^ c) >> j) == 1) for j in range(n_levels)]
    return jnp.asarray(np.stack(masks).astype(np.float32))


def _layer_spec(arr, layer):
    nd = arr.ndim - 1
    return pl.BlockSpec((None,) + arr.shape[1:], lambda b, s: (layer,) + (0,) * nd,
                        pipeline_mode=pl.Buffered(1))


def _const_spec(arr):
    nd = arr.ndim
    return pl.BlockSpec(arr.shape, lambda b, s: (0,) * nd, pipeline_mode=pl.Buffered(1))


def _mixer_params(norm_mix, w_in, gdn_conv_w, gdn_dt_bias, gdn_a_log, gdn_norm, w_gdn_out,
                  cc_glu_b, cc_dw_w, cc_dw_b, cc_ln_w, cc_ln_b, w_cc_out, w_xa_out, gate_b, w_o):
    p0 = QKV_A
    p1 = p0 + GDN_HEADS
    p2 = p1 + GDN_HEADS
    p3 = p2 + V_A
    p4 = p3 + 2 * CONV_CH
    p5 = p4 + XA_W
    w_ab = jnp.pad(w_in[:, :, p0:p2], ((0, 0), (0, 0), (0, LANES - 2 * GDN_HEADS)))
    pad_row = lambda v: jnp.pad(v, ((0, 0), (0, LANES - v.shape[1])))[:, None, :]
    row = lambda v: v[:, None, :]
    return [
        row(norm_mix), w_in[:, :, :p0].astype(BF16), w_ab.astype(BF16), w_in[:, :, p2:p3].astype(BF16),
        w_in[:, :, p3:p4].astype(BF16), w_in[:, :, p4:p5].astype(BF16), w_in[:, :, p5:].astype(BF16),
        gdn_conv_w, pad_row(gdn_a_log), pad_row(gdn_dt_bias), row(gdn_norm), w_gdn_out.astype(BF16),
        row(cc_glu_b), cc_dw_w, row(cc_dw_b), row(cc_ln_w), row(cc_ln_b), w_cc_out.astype(BF16),
        w_xa_out.astype(BF16), row(gate_b), w_o.astype(BF16),
    ]


def _mixer(x, kvm, params, lvl, layer):
    bn, sn, d = x.shape
    tm = MIXER_ROWS
    mn, kvw = kvm.shape[2], kvm.shape[3]
    return pl.pallas_call(
        _mixer_kernel,
        grid=(bn, sn // tm),
        in_specs=[pl.BlockSpec((1, tm, d), lambda b, s: (b, s, 0)),
                  pl.BlockSpec((None, 1, mn, kvw), lambda b, s: (layer, b, 0, 0))]
                 + [_layer_spec(p, layer) for p in params] + [_const_spec(lvl)],
        out_specs=pl.BlockSpec((1, tm, d), lambda b, s: (b, s, 0)),
        out_shape=jax.ShapeDtypeStruct(x.shape, x.dtype),
        scratch_shapes=[
            pltpu.VMEM((tm, d), BF16),
            pltpu.VMEM((QKV_A // LANES, QKV_HALO + tm, LANES), F32),
            pltpu.VMEM((tm, QK_A), F32),
            pltpu.VMEM((tm, QK_A), F32),
            pltpu.VMEM((tm, V_A), F32),
            pltpu.VMEM((tm, V_A), F32),
            pltpu.VMEM((CONV_CH // LANES, CC_HALO + tm, LANES), F32),
            pltpu.VMEM((tm, CONV_CH), F32),
            pltpu.VMEM((tm, XA_W), F32),
            pltpu.VMEM((GDN_HEADS, GDN_DK, GDN_DV), F32),
            pltpu.VMEM((tm, N_BRANCH * d), F32),
            pltpu.VMEM((N_BRANCH, tm, d), F32),
        ],
        compiler_params=pltpu.CompilerParams(
            dimension_semantics=("arbitrary", "arbitrary"),
            vmem_limit_bytes=VMEM_LIMIT_BYTES),
        name="mixer",
    )(x, kvm, *params, lvl)


def _ffn_kernel(x_ref, nffn_ref, wup_ref, dww_ref, dwb_ref, wdown_ref, nfin_ref, out_ref,
                gbuf_s, act_s, *, final):
    tm = x_ref.shape[1]
    f = dww_ref.shape[1]

    @pl.when(pl.program_id(1) == 0)
    def _():
        gbuf_s[:, 0:FFN_HALO, :] = jnp.zeros((gbuf_s.shape[0], FFN_HALO, LANES), F32)

    x = x_ref[0]
    hb = _rmsnorm(x, nffn_ref[...]).astype(BF16)
    _to_slabs(gbuf_s, FFN_HALO, _mm(hb, wup_ref[:, :f]))
    act_s[...] = _mm(hb, wup_ref[:, f:])

    def emit(r0, c0, blk):
        rows, cols = slice(r0, r0 + CONV_ROWS), slice(c0, c0 + LANES)
        act_s[rows, cols] = _silu(blk + dwb_ref[:, cols]) * act_s[rows, cols]

    for _ in _causal_dwconv(gbuf_s, dww_ref, FFN_HALO, tm, FFN_CONV, emit):
        pass
    y = x + _mm(act_s[...], wdown_ref[...])
    if final:
        y = _rmsnorm(y, nfin_ref[...])
    out_ref[0] = y


def _ffn_params(norm_ffn, w_up, ffn_dw_w, ffn_dw_b, w_down):
    row = lambda v: v[:, None, :]
    return [row(norm_ffn), w_up.astype(BF16), ffn_dw_w, row(ffn_dw_b), w_down.astype(BF16)]


def _ffn(x, params, norm_final, layer, final):
    bn, sn, d = x.shape
    tm = FFN_ROWS
    f = params[4].shape[1]
    nfin = norm_final.reshape(1, d)
    return pl.pallas_call(
        functools.partial(_ffn_kernel, final=final),
        grid=(bn, sn // tm),
        in_specs=[pl.BlockSpec((1, tm, d), lambda b, s: (b, s, 0))]
                 + [_layer_spec(p, layer) for p in params] + [_const_spec(nfin)],
        out_specs=pl.BlockSpec((1, tm, d), lambda b, s: (b, s, 0)),
        out_shape=jax.ShapeDtypeStruct(x.shape, x.dtype),
        scratch_shapes=[
            pltpu.VMEM((f // LANES, FFN_HALO + tm, LANES), F32),
            pltpu.VMEM((tm, f), F32),
        ],
        compiler_params=pltpu.CompilerParams(
            dimension_semantics=("arbitrary", "arbitrary"),
            vmem_limit_bytes=VMEM_LIMIT_BYTES),
        name="ffn",
    )(x, *params, nfin)


def kernel(x, mem, norm_mix, w_in, gdn_conv_w, gdn_dt_bias, gdn_a_log, gdn_norm, w_gdn_out, cc_glu_b, cc_dw_w, cc_dw_b, cc_ln_w, cc_ln_b, w_cc_out, mem_norm, w_mem_kv, w_xa_out, gate_b, w_o, norm_ffn, w_up, ffn_dw_w, ffn_dw_b, w_down, norm_final):
    n_layers = w_in.shape[0]
    assert x.shape[1] % MIXER_ROWS == 0 and x.shape[1] % FFN_ROWS == 0
    assert MIXER_ROWS % GDN_CHUNK == 0 and MIXER_ROWS % CONV_ROWS == 0 and FFN_ROWS % CONV_ROWS == 0
    kvm = _memkv(mem, mem_norm, w_mem_kv)
    mixer_params = _mixer_params(norm_mix, w_in, gdn_conv_w, gdn_dt_bias, gdn_a_log, gdn_norm,
                                 w_gdn_out, cc_glu_b, cc_dw_w, cc_dw_b, cc_ln_w, cc_ln_b, w_cc_out,
                                 w_xa_out, gate_b, w_o)
    ffn_params = _ffn_params(norm_ffn, w_up, ffn_dw_w, ffn_dw_b, w_down)
    lvl = _level_masks(MIXER_ROWS)
    for l in range(n_layers):
        x = _mixer(x, kvm, mixer_params, lvl, l)
        x = _ffn(x, ffn_params, norm_final, l, final=(l == n_layers - 1))
    return x
```

```python
import functools

import numpy as np
import jax
import jax.numpy as jnp
from jax import lax
from jax.experimental import pallas as pl
from jax.experimental.pallas import tpu as pltpu

EPS = 1e-6
GDN_HEADS = 4
GDN_DK = 128
GDN_DV = 128
GDN_CONV = 4
GDN_CHUNK = 64
QK_A = GDN_HEADS * GDN_DK
V_A = GDN_HEADS * GDN_DV
QKV_A = 2 * QK_A + V_A
CONV_CH = 512
CONV_K = 31
XA_HEADS = 4
XA_DH = 128
XA_W = XA_HEADS * XA_DH
N_BRANCH = 3
FFN_CONV = 3

LANES = 128
SUBLANES = 8
MIXER_ROWS = 256
FFN_ROWS = 512
CONV_ROWS = 64
GATE_COLS = 256
FILL_RATIO = 2
QKV_HALO = SUBLANES
CC_HALO = 4 * SUBLANES
FFN_HALO = SUBLANES
VMEM_LIMIT_BYTES = 56 * 1024 * 1024

BF16 = jnp.bfloat16
F32 = jnp.float32


def _mm(a, b):
    return jnp.dot(a.astype(BF16), b.astype(BF16), preferred_element_type=F32)


def _mm_nt(a, b):
    return lax.dot_general(a.astype(BF16), b.astype(BF16), (((1,), (1,)), ((), ())),
                           preferred_element_type=F32)


def _mm_tn(a, b):
    return lax.dot_general(a.astype(BF16), b.astype(BF16), (((0,), (0,)), ((), ())),
                           preferred_element_type=F32)


def _rmsnorm(x, w_row):
    ms = jnp.mean(x * x, axis=-1, keepdims=True)
    return x * lax.rsqrt(ms + EPS) * w_row


def _sigmoid(x):
    return 0.5 * jnp.tanh(0.5 * x) + 0.5


def _silu(x):
    return x * _sigmoid(x)


def _softplus(x):
    return jnp.maximum(x, 0.0) + jnp.log(1.0 + jnp.exp(-jnp.abs(x)))


def _interleave(main, fillers, ratio):
    live = list(fillers)

    def fill(n):
        while n > 0 and live:
            for g in list(live):
                if n == 0:
                    break
                try:
                    next(g)
                    n -= 1
                except StopIteration:
                    live.remove(g)

    for _ in main:
        fill(ratio)
    fill(float("inf"))


def _to_slabs(buf_ref, halo, val):
    for cb in range(buf_ref.shape[0]):
        buf_ref[cb, halo:halo + val.shape[0], :] = val[:, cb * LANES:(cb + 1) * LANES]


def _causal_dwconv(buf_ref, w_ref, halo, n_rows, n_taps, emit):
    for cb in range(buf_ref.shape[0]):
        c0 = cb * LANES
        for r0 in range(0, n_rows, CONV_ROWS):
            acc = None
            for k in range(n_taps):
                start = halo - (n_taps - 1) + k + r0
                term = w_ref[k:k + 1, c0:c0 + LANES] * buf_ref[cb, start:start + CONV_ROWS, :]
                acc = term if acc is None else acc + term
            emit(r0, c0, acc)
            yield
    buf_ref[:, 0:halo, :] = buf_ref[:, n_rows:n_rows + halo, :]


def _memkv_kernel(mem_ref, nrm_ref, w_ref, out_ref):
    m = _rmsnorm(mem_ref[0], nrm_ref[0])
    out_ref[0, 0] = _mm(m, w_ref[0]).astype(out_ref.dtype)


def _memkv(mem, mem_norm, w_mem_kv):
    n_layers = w_mem_kv.shape[0]
    bn, mn, d = mem.shape
    n_out = w_mem_kv.shape[2]
    return pl.pallas_call(
        _memkv_kernel,
        grid=(n_layers, bn),
        in_specs=[
            pl.BlockSpec((1, mn, d), lambda l, b: (b, 0, 0)),
            pl.BlockSpec((1, 1, d), lambda l, b: (l, 0, 0)),
            pl.BlockSpec((1, d, n_out), lambda l, b: (l, 0, 0)),
        ],
        out_specs=pl.BlockSpec((1, 1, mn, n_out), lambda l, b: (l, b, 0, 0)),
        out_shape=jax.ShapeDtypeStruct((n_layers, bn, mn, n_out), BF16),
        compiler_params=pltpu.CompilerParams(
            dimension_semantics=("arbitrary", "arbitrary"),
            vmem_limit_bytes=VMEM_LIMIT_BYTES),
        name="mem_kv",
    )(mem, mem_norm.reshape(n_layers, 1, d), w_mem_kv.astype(BF16))


def _mixer_kernel(x_ref, kvm_ref, nmix_ref, wqkv_ref, wab_ref, wz_ref, wglu_ref, wqc_ref,
                  wgate_ref, convw_ref, alog_ref, dtb_ref, gnorm_ref, wgdn_ref,
                  glub_ref, dww_ref, dwb_ref, lnw_ref, lnb_ref, wcc_ref,
                  wxa_ref, gateb_ref, wo_ref, lvl_ref,
                  out_ref,
                  h_s, qkv_s, q_s, k_s, v_s, o_s, ubuf_s, conv_s, att_s, state_s, gate_s, y_s):
    tm = x_ref.shape[1]
    d = x_ref.shape[2]

    @pl.when(pl.program_id(1) == 0)
    def _():
        qkv_s[:, 0:QKV_HALO, :] = jnp.zeros((qkv_s.shape[0], QKV_HALO, LANES), F32)
        ubuf_s[:, 0:CC_HALO, :] = jnp.zeros((ubuf_s.shape[0], CC_HALO, LANES), F32)
        state_s[...] = jnp.zeros(state_s.shape, F32)

    x = x_ref[0]
    h_s[...] = _rmsnorm(x, nmix_ref[...]).astype(BF16)

    gate_cols_written = set()

    def gates():
        for c0 in range(0, N_BRANCH * d, GATE_COLS):
            cols = slice(c0, c0 + GATE_COLS)
            gate_s[:, cols] = _sigmoid(_mm(h_s[...], wgate_ref[:, cols]) + gateb_ref[:, cols])
            gate_cols_written.add(c0)
            yield

    def gated(branch, y):
        assert all(c0 in gate_cols_written for c0 in range(branch * d, (branch + 1) * d, GATE_COLS))
        y_s[branch] = gate_s[:, branch * d:(branch + 1) * d] * y

    def branch_a():
        ab = _mm(h_s[...], wab_ref[...])
        g = -jnp.exp(alog_ref[...]) * _softplus(ab + dtb_ref[...])
        beta = _sigmoid(ab)

        n_chunks = tm // GDN_CHUNK
        chunk_bits = GDN_CHUNK.bit_length() - 1
        ri = lax.broadcasted_iota(jnp.int32, (tm, tm), 0)
        ci = lax.broadcasted_iota(jnp.int32, (tm, tm), 1)
        same_chunk = (ri >> chunk_bits) == (ci >> chunk_bits)
        causal = same_chunk & (ri >= ci)
        strict = same_chunk & (ri > ci)
        eye = (ri == ci).astype(F32)

        tri = causal.astype(BF16)
        g_hi = g.astype(BF16)
        g_r1 = g - g_hi.astype(F32)
        g_mid = g_r1.astype(BF16)
        g_lo = (g_r1 - g_mid.astype(F32)).astype(BF16)
        gc = _mm(tri, g_hi) + _mm(tri, g_mid) + _mm(tri, g_lo)
        gct = gc.T
        eg = jnp.exp(gc)
        egl_parts, eg_last = [], []
        for c in range(n_chunks):
            gl = gc[(c + 1) * GDN_CHUNK - 1:(c + 1) * GDN_CHUNK, :]
            egl_parts.append(jnp.exp(gl - gc[c * GDN_CHUNK:(c + 1) * GDN_CHUNK, :]))
            eg_last.append(jnp.exp(gl))
        egl = jnp.concatenate(egl_parts, axis=0)
        heads = range(GDN_HEADS)
        decay = [jnp.exp(jnp.where(causal, gc[:, hh:hh + 1] - gct[hh:hh + 1, :], -jnp.inf))
                 for hh in heads]
        yield

        _to_slabs(qkv_s, QKV_HALO, _mm(h_s[...], wqkv_ref[...]))
        yield

        def emit_qkv(r0, c0, blk):
            blk = _silu(blk)
            rows = slice(r0, r0 + CONV_ROWS)
            if c0 < 2 * QK_A:
                blk = blk * lax.rsqrt(jnp.sum(blk * blk, axis=-1, keepdims=True) + EPS)
            if c0 < QK_A:
                q_s[rows, c0:c0 + LANES] = blk * (GDN_DK ** -0.5)
            elif c0 < 2 * QK_A:
                k_s[rows, c0 - QK_A:c0 - QK_A + LANES] = blk
            else:
                v_s[rows, c0 - 2 * QK_A:c0 - 2 * QK_A + LANES] = blk

        for _ in _causal_dwconv(qkv_s, convw_ref, QKV_HALO, tm, GDN_CONV, emit_qkv):
            pass
        yield

        lanes = [slice(hh * GDN_DK, (hh + 1) * GDN_DK) for hh in heads]
        q_h = [q_s[:, ls] for ls in lanes]
        k_h = [k_s[:, ls] for ls in lanes]
        bcol = [beta[:, GDN_HEADS + hh:GDN_HEADS + hh + 1] for hh in heads]
        egcol = [eg[:, hh:hh + 1] for hh in heads]
        kb = [k_h[hh] * bcol[hh] for hh in heads]
        a_mat = [jnp.where(strict, _mm_nt(kb[hh], k_h[hh]) * decay[hh], 0.0) for hh in heads]
        yield
        qk = [jnp.where(causal, _mm_nt(q_h[hh], k_h[hh]) * decay[hh], 0.0) for hh in heads]
        t_mat = [eye - a_mat[hh] * lvl_ref[0] for hh in heads]
        yield
        for j in range(1, chunk_bits):
            half = 1 << j
            if half < SUBLANES:
                xt = [_mm(a_mat[hh] * lvl_ref[j], t_mat[hh]) for hh in heads]
                yield
                t_mat = [t_mat[hh] - _mm(t_mat[hh], xt[hh]) for hh in heads]
                yield
                continue
            lower = [slice(r, r + half) for r in range(half, tm, 2 * half)]
            upper = [slice(r, r + half) for r in range(0, tm, 2 * half)]
            pick = lambda m: jnp.concatenate([m[rows] for rows in lower], axis=0)
            sel = jnp.concatenate([lvl_ref[j, rows, :] for rows in lower], axis=0)
            xl = [_mm(pick(a_mat[hh]) * sel, t_mat[hh]) for hh in heads]
            yield
            zeros = jnp.zeros((half, tm), F32)
            upd = []
            for hh in heads:
                x_full = jnp.concatenate(
                    [blk for i in range(len(lower)) for blk in (zeros, xl[hh][i * half:(i + 1) * half])],
                    axis=0)
                upd.append(_mm(pick(t_mat[hh]), x_full))
            t_mat = [jnp.concatenate(
                [blk for i in range(len(lower))
                 for blk in (t_mat[hh][upper[i]], t_mat[hh][lower[i]] - upd[hh][i * half:(i + 1) * half])],
                axis=0) for hh in heads]
            yield
        uw = [_mm(t_mat[hh], jnp.concatenate([v_s[:, lanes[hh]] * bcol[hh], kb[hh] * egcol[hh]], axis=1))
              for hh in heads]
        yield
        loc = [_mm(qk[hh], uw[hh]) for hh in heads]
        k_dec = [k_h[hh] * egl[:, hh:hh + 1] for hh in heads]
        chunk_rows = [slice(c * GDN_CHUNK, (c + 1) * GDN_CHUNK) for c in range(n_chunks)]
        kuw = [[_mm_tn(k_dec[hh][rows], uw[hh][rows]) for rows in chunk_rows] for hh in heads]
        yield
        q_til = [q_h[hh] * egcol[hh] - loc[hh][:, GDN_DV:] for hh in heads]
        st = [state_s[hh] for hh in heads]
        o_parts = [[] for _ in heads]
        for c, rows in enumerate(chunk_rows):
            for hh in heads:
                r = _mm(jnp.concatenate([kuw[hh][c][:, GDN_DV:], q_til[hh][rows]], axis=0), st[hh])
                o_parts[hh].append(r[GDN_DK:])
                st[hh] = st[hh] * eg_last[c][:, hh:hh + 1] + kuw[hh][c][:, :GDN_DV] - r[:GDN_DK]
            yield
        z = _mm(h_s[...], wz_ref[...])
        for hh in heads:
            state_s[hh] = st[hh]
            o = jnp.concatenate(o_parts[hh], axis=0) + loc[hh][:, :GDN_DV]
            o_s[:, lanes[hh]] = _rmsnorm(o, gnorm_ref[...]) * _silu(z[:, lanes[hh]])
        yield
        gated(0, _mm(o_s[...], wgdn_ref[...]))

    def branch_b():
        glu = _mm(h_s[...], wglu_ref[...]) + glub_ref[...]
        _to_slabs(ubuf_s, CC_HALO, glu[:, :CONV_CH] * _sigmoid(glu[:, CONV_CH:]))
        yield

        def emit_cc(r0, c0, blk):
            conv_s[r0:r0 + CONV_ROWS, c0:c0 + LANES] = blk + dwb_ref[:, c0:c0 + LANES]

        yield from _causal_dwconv(ubuf_s, dww_ref, CC_HALO, tm, CONV_K, emit_cc)
        for r0 in range(0, tm, CONV_ROWS):
            blk = conv_s[r0:r0 + CONV_ROWS, :]
            mu = jnp.mean(blk, axis=-1, keepdims=True)
            cen = blk - mu
            var = jnp.mean(cen * cen, axis=-1, keepdims=True)
            conv_s[r0:r0 + CONV_ROWS, :] = _silu(cen * lax.rsqrt(var + EPS) * lnw_ref[...] + lnb_ref[...])
            yield
        gated(1, _mm(conv_s[...], wcc_ref[...]))

    def branch_c():
        qc = _mm(h_s[...], wqc_ref[...])
        yield
        for hh in range(XA_HEADS):
            ls = slice(hh * XA_DH, (hh + 1) * XA_DH)
            k_m = kvm_ref[0, :, hh * XA_DH:(hh + 1) * XA_DH]
            v_m = kvm_ref[0, :, XA_W + hh * XA_DH:XA_W + (hh + 1) * XA_DH]
            sc = _mm_nt(qc[:, ls], k_m) * (XA_DH ** -0.5)
            ex = jnp.exp(sc - jnp.max(sc, axis=-1, keepdims=True))
            att_s[:, ls] = _mm(ex, v_m) * (1.0 / jnp.sum(ex, axis=-1, keepdims=True))
            yield
        gated(2, _mm(att_s[...], wxa_ref[...]))

    _interleave(branch_a(), [gates(), branch_b(), branch_c()], FILL_RATIO)
    merged = y_s[0] + y_s[1] + y_s[2]
    out_ref[0] = x + _mm(merged, wo_ref[...])


def _level_masks(tm):
    r = np.arange(tm)[:, None]
    c = np.arange(tm)[None, :]
    same_chunk = (r // GDN_CHUNK) == (c // GDN_CHUNK)
    n_levels = GDN_CHUNK.bit_length() - 1
    masks = [same_chunk & (r > c) & (((r ^ c) >> j) == 1) for j in range(n_levels)]
    return jnp.asarray(np.stack(masks).astype(np.float32))


def _layer_spec(arr, layer):
    nd = arr.ndim - 1
    return pl.BlockSpec((None,) + arr.shape[1:], lambda b, s: (layer,) + (0,) * nd,
                        pipeline_mode=pl.Buffered(1))


def _const_spec(arr):
    nd = arr.ndim
    return pl.BlockSpec(arr.shape, lambda b, s: (0,) * nd, pipeline_mode=pl.Buffered(1))


def _mixer_params(norm_mix, w_in, gdn_conv_w, gdn_dt_bias, gdn_a_log, gdn_norm, w_gdn_out,
                  cc_glu_b, cc_dw_w, cc_dw_b, cc_ln_w, cc_ln_b, w_cc_out, w_xa_out, gate_b, w_o):
    p0 = QKV_A
    p1 = p0 + GDN_HEADS
    p2 = p1 + GDN_HEADS
    p3 = p2 + V_A
    p4 = p3 + 2 * CONV_CH
    p5 = p4 + XA_W
    w_ab = jnp.pad(w_in[:, :, p0:p2], ((0, 0), (0, 0), (0, LANES - 2 * GDN_HEADS)))
    pad_row = lambda v: jnp.pad(v, ((0, 0), (0, LANES - v.shape[1])))[:, None, :]
    row = lambda v: v[:, None, :]
    return [
        row(norm_mix), w_in[:, :, :p0].astype(BF16), w_ab.astype(BF16), w_in[:, :, p2:p3].astype(BF16),
        w_in[:, :, p3:p4].astype(BF16), w_in[:, :, p4:p5].astype(BF16), w_in[:, :, p5:].astype(BF16),
        gdn_conv_w, pad_row(gdn_a_log), pad_row(gdn_dt_bias), row(gdn_norm), w_gdn_out.astype(BF16),
        row(cc_glu_b), cc_dw_w, row(cc_dw_b), row(cc_ln_w), row(cc_ln_b), w_cc_out.astype(BF16),
        w_xa_out.astype(BF16), row(gate_b), w_o.astype(BF16),
    ]


def _mixer(x, kvm, params, lvl, layer):
    bn, sn, d = x.shape
    tm = MIXER_ROWS
    mn, kvw = kvm.shape[2], kvm.shape[3]
    return pl.pallas_call(
        _mixer_kernel,
        grid=(bn, sn // tm),
        in_specs=[pl.BlockSpec((1, tm, d), lambda b, s: (b, s, 0)),
                  pl.BlockSpec((None, 1, mn, kvw), lambda b, s: (layer, b, 0, 0))]
                 + [_layer_spec(p, layer) for p in params] + [_const_spec(lvl)],
        out_specs=pl.BlockSpec((1, tm, d), lambda b, s: (b, s, 0)),
        out_shape=jax.ShapeDtypeStruct(x.shape, x.dtype),
        scratch_shapes=[
            pltpu.VMEM((tm, d), BF16),
            pltpu.VMEM((QKV_A // LANES, QKV_HALO + tm, LANES), F32),
            pltpu.VMEM((tm, QK_A), F32),
            pltpu.VMEM((tm, QK_A), F32),
            pltpu.VMEM((tm, V_A), F32),
            pltpu.VMEM((tm, V_A), F32),
            pltpu.VMEM((CONV_CH // LANES, CC_HALO + tm, LANES), F32),
            pltpu.VMEM((tm, CONV_CH), F32),
            pltpu.VMEM((tm, XA_W), F32),
            pltpu.VMEM((GDN_HEADS, GDN_DK, GDN_DV), F32),
            pltpu.VMEM((tm, N_BRANCH * d), F32),
            pltpu.VMEM((N_BRANCH, tm, d), F32),
        ],
        compiler_params=pltpu.CompilerParams(
            dimension_semantics=("arbitrary", "arbitrary"),
            vmem_limit_bytes=VMEM_LIMIT_BYTES),
        name="mixer",
    )(x, kvm, *params, lvl)


def _ffn_kernel(x_ref, nffn_ref, wup_ref, dww_ref, dwb_ref, wdown_ref, nfin_ref, out_ref,
                gbuf_s, act_s, *, final):
    tm = x_ref.shape[1]
    f = dww_ref.shape[1]

    @pl.when(pl.program_id(1) == 0)
    def _():
        gbuf_s[:, 0:FFN_HALO, :] = jnp.zeros((gbuf_s.shape[0], FFN_HALO, LANES), F32)

    x = x_ref[0]
    hb = _rmsnorm(x, nffn_ref[...]).astype(BF16)
    _to_slabs(gbuf_s, FFN_HALO, _mm(hb, wup_ref[:, :f]))
    act_s[...] = _mm(hb, wup_ref[:, f:])

    def emit(r0, c0, blk):
        rows, cols = slice(r0, r0 + CONV_ROWS), slice(c0, c0 + LANES)
        act_s[rows, cols] = _silu(blk + dwb_ref[:, cols]) * act_s[rows, cols]

    for _ in _causal_dwconv(gbuf_s, dww_ref, FFN_HALO, tm, FFN_CONV, emit):
        pass
    y = x + _mm(act_s[...], wdown_ref[...])
    if final:
        y = _rmsnorm(y, nfin_ref[...])
    out_ref[0] = y


def _ffn_params(norm_ffn, w_up, ffn_dw_w, ffn_dw_b, w_down):
    row = lambda v: v[:, None, :]
    return [row(norm_ffn), w_up.astype(BF16), ffn_dw_w, row(ffn_dw_b), w_down.astype(BF16)]


def _ffn(x, params, norm_final, layer, final):
    bn, sn, d = x.shape
    tm = FFN_ROWS
    f = params[4].shape[1]
    nfin = norm_final.reshape(1, d)
    return pl.pallas_call(
        functools.partial(_ffn_kernel, final=final),
        grid=(bn, sn // tm),
        in_specs=[pl.BlockSpec((1, tm, d), lambda b, s: (b, s, 0))]
                 + [_layer_spec(p, layer) for p in params] + [_const_spec(nfin)],
        out_specs=pl.BlockSpec((1, tm, d), lambda b, s: (b, s, 0)),
        out_shape=jax.ShapeDtypeStruct(x.shape, x.dtype),
        scratch_shapes=[
            pltpu.VMEM((f // LANES, FFN_HALO + tm, LANES), F32),
            pltpu.VMEM((tm, f), F32),
        ],
        compiler_params=pltpu.CompilerParams(
            dimension_semantics=("arbitrary", "arbitrary"),
            vmem_limit_bytes=VMEM_LIMIT_BYTES),
        name="ffn",
    )(x, *params, nfin)


def kernel(x, mem, norm_mix, w_in, gdn_conv_w, gdn_dt_bias, gdn_a_log, gdn_norm, w_gdn_out, cc_glu_b, cc_dw_w, cc_dw_b, cc_ln_w, cc_ln_b, w_cc_out, mem_norm, w_mem_kv, w_xa_out, gate_b, w_o, norm_ffn, w_up, ffn_dw_w, ffn_dw_b, w_down, norm_final):
    n_layers = w_in.shape[0]
    assert x.shape[1] % MIXER_ROWS == 0 and x.shape[1] % FFN_ROWS == 0
    assert MIXER_ROWS % GDN_CHUNK == 0 and MIXER_ROWS % CONV_ROWS == 0 and FFN_ROWS % CONV_ROWS == 0
    kvm = _memkv(mem, mem_norm, w_mem_kv)
    mixer_params = _mixer_params(norm_mix, w_in, gdn_conv_w, gdn_dt_bias, gdn_a_log, gdn_norm,
                                 w_gdn_out, cc_glu_b, cc_dw_w, cc_dw_b, cc_ln_w, cc_ln_b, w_cc_out,
                                 w_xa_out, gate_b, w_o)
    ffn_params = _ffn_params(norm_ffn, w_up, ffn_dw_w, ffn_dw_b, w_down)
    lvl = _level_masks(MIXER_ROWS)
    for l in range(n_layers):
        x = _mixer(x, kvm, mixer_params, lvl, l)
        x = _ffn(x, ffn_params, norm_final, l, final=(l == n_layers - 1))
    return x
```

```python
import functools

import numpy as np
import jax
import jax.numpy as jnp
from jax import lax
from jax.experimental import pallas as pl
from jax.experimental.pallas import tpu as pltpu

EPS = 1e-6
GDN_HEADS = 4
GDN_DK = 128
GDN_DV = 128
GDN_CONV = 4
GDN_CHUNK = 64
QK_A = GDN_HEADS * GDN_DK
V_A = GDN_HEADS * GDN_DV
QKV_A = 2 * QK_A + V_A
CONV_CH = 512
CONV_K = 31
XA_HEADS = 4
XA_DH = 128
XA_W = XA_HEADS * XA_DH
N_BRANCH = 3
FFN_CONV = 3

LANES = 128
SUBLANES = 8
MIXER_ROWS = 256
FFN_ROWS = 512
CONV_ROWS = 64
GATE_COLS = 256
FILL_RATIO = 2
QKV_HALO = SUBLANES
CC_HALO = 4 * SUBLANES
FFN_HALO = SUBLANES
VMEM_LIMIT_BYTES = 56 * 1024 * 1024

BF16 = jnp.bfloat16
F32 = jnp.float32


def _mm(a, b):
    return jnp.dot(a.astype(BF16), b.astype(BF16), preferred_element_type=F32)


def _mm_nt(a, b):
    return lax.dot_general(a.astype(BF16), b.astype(BF16), (((1,), (1,)), ((), ())),
                           preferred_element_type=F32)


def _mm_tn(a, b):
    return lax.dot_general(a.astype(BF16), b.astype(BF16), (((0,), (0,)), ((), ())),
                           preferred_element_type=F32)


def _rmsnorm(x, w_row):
    ms = jnp.mean(x * x, axis=-1, keepdims=True)
    return x * lax.rsqrt(ms + EPS) * w_row


def _sigmoid(x):
    return 0.5 * jnp.tanh(0.5 * x) + 0.5


def _silu(x):
    return x * _sigmoid(x)


def _softplus(x):
    return jnp.maximum(x, 0.0) + jnp.log(1.0 + jnp.exp(-jnp.abs(x)))


def _interleave(main, fillers, ratio):
    live = list(fillers)

    def fill(n):
        while n > 0 and live:
            for g in list(live):
                if n == 0:
                    break
                try:
                    next(g)
                    n -= 1
                except StopIteration:
                    live.remove(g)

    for _ in main:
        fill(ratio)
    fill(float("inf"))


def _to_slabs(buf_ref, halo, val):
    for cb in range(buf_ref.shape[0]):
        buf_ref[cb, halo:halo + val.shape[0], :] = val[:, cb * LANES:(cb + 1) * LANES]


def _causal_dwconv(buf_ref, w_ref, halo, n_rows, n_taps, emit):
    for cb in range(buf_ref.shape[0]):
        c0 = cb * LANES
        for r0 in range(0, n_rows, CONV_ROWS):
            acc = None
            for k in range(n_taps):
                start = halo - (n_taps - 1) + k + r0
                term = w_ref[k:k + 1, c0:c0 + LANES] * buf_ref[cb, start:start + CONV_ROWS, :]
                acc = term if acc is None else acc + term
            emit(r0, c0, acc)
            yield
    buf_ref[:, 0:halo, :] = buf_ref[:, n_rows:n_rows + halo, :]


def _memkv_kernel(mem_ref, nrm_ref, w_ref, out_ref):
    m = _rmsnorm(mem_ref[0], nrm_ref[0])
    out_ref[0, 0] = _mm(m, w_ref[0]).astype(out_ref.dtype)


def _memkv(mem, mem_norm, w_mem_kv):
    n_layers = w_mem_kv.shape[0]
    bn, mn, d = mem.shape
    n_out = w_mem_kv.shape[2]
    return pl.pallas_call(
        _memkv_kernel,
        grid=(n_layers, bn),
        in_specs=[
            pl.BlockSpec((1, mn, d), lambda l, b: (b, 0, 0)),
            pl.BlockSpec((1, 1, d), lambda l, b: (l, 0, 0)),
            pl.BlockSpec((1, d, n_out), lambda l, b: (l, 0, 0)),
        ],
        out_specs=pl.BlockSpec((1, 1, mn, n_out), lambda l, b: (l, b, 0, 0)),
        out_shape=jax.ShapeDtypeStruct((n_layers, bn, mn, n_out), BF16),
        compiler_params=pltpu.CompilerParams(
            dimension_semantics=("arbitrary", "arbitrary"),
            vmem_limit_bytes=VMEM_LIMIT_BYTES),
        name="mem_kv",
    )(mem, mem_norm.reshape(n_layers, 1, d), w_mem_kv.astype(BF16))


def _mixer_kernel(x_ref, xn_ref, kvm_ref, nmix_ref, wqkv_ref, wab_ref, wz_ref, wglu_ref, wqc_ref,
                  wgate_ref, convw_ref, alog_ref, dtb_ref, gnorm_ref, wgdn_ref,
                  glub_ref, dww_ref, dwb_ref, lnw_ref, lnb_ref, wcc_ref,
                  wxa_ref, gateb_ref, wo_ref, lvl_ref,
                  out_ref,
                  h_s, hn_s, qkv_s, q_s, k_s, v_s, o_s, ubuf_s, conv_s, att_s, state_s, gate_s, y_s):
    tm = x_ref.shape[1]
    d = x_ref.shape[2]

    @pl.when(pl.program_id(1) == 0)
    def _():
        qkv_s[:, 0:QKV_HALO, :] = jnp.zeros((qkv_s.shape[0], QKV_HALO, LANES), F32)
        ubuf_s[:, 0:CC_HALO, :] = jnp.zeros((ubuf_s.shape[0], CC_HALO, LANES), F32)
        state_s[...] = jnp.zeros(state_s.shape, F32)
        h_s[...] = _rmsnorm(x_ref[0], nmix_ref[...]).astype(BF16)

    @pl.when(pl.program_id(1) != 0)
    def _():
        h_s[...] = hn_s[...]

    x = x_ref[0]

    gate_cols_written = set()

    def gates():
        for c0 in range(0, N_BRANCH * d, GATE_COLS):
            cols = slice(c0, c0 + GATE_COLS)
            gate_s[:, cols] = _sigmoid(_mm(h_s[...], wgate_ref[:, cols]) + gateb_ref[:, cols])
            gate_cols_written.add(c0)
            yield

    def gated(branch, y):
        assert all(c0 in gate_cols_written for c0 in range(branch * d, (branch + 1) * d, GATE_COLS))
        y_s[branch] = gate_s[:, branch * d:(branch + 1) * d] * y

    def branch_a():
        ab = _mm(h_s[...], wab_ref[...])
        g = -jnp.exp(alog_ref[...]) * _softplus(ab + dtb_ref[...])
        beta = _sigmoid(ab)

        n_chunks = tm // GDN_CHUNK
        chunk_bits = GDN_CHUNK.bit_length() - 1
        ri = lax.broadcasted_iota(jnp.int32, (tm, tm), 0)
        ci = lax.broadcasted_iota(jnp.int32, (tm, tm), 1)
        same_chunk = (ri >> chunk_bits) == (ci >> chunk_bits)
        causal = same_chunk & (ri >= ci)
        strict = same_chunk & (ri > ci)
        eye = (ri == ci).astype(F32)

        tri = causal.astype(BF16)
        g_hi = g.astype(BF16)
        g_r1 = g - g_hi.astype(F32)
        g_mid = g_r1.astype(BF16)
        g_lo = (g_r1 - g_mid.astype(F32)).astype(BF16)
        gc = _mm(tri, g_hi) + _mm(tri, g_mid) + _mm(tri, g_lo)
        gct = gc.T
        eg = jnp.exp(gc)
        egl_parts, eg_last = [], []
        for c in range(n_chunks):
            gl = gc[(c + 1) * GDN_CHUNK - 1:(c + 1) * GDN_CHUNK, :]
            egl_parts.append(jnp.exp(gl - gc[c * GDN_CHUNK:(c + 1) * GDN_CHUNK, :]))
            eg_last.append(jnp.exp(gl))
        egl = jnp.concatenate(egl_parts, axis=0)
        heads = range(GDN_HEADS)
        decay = [jnp.exp(jnp.where(causal, gc[:, hh:hh + 1] - gct[hh:hh + 1, :], -jnp.inf))
                 for hh in heads]
        yield

        _to_slabs(qkv_s, QKV_HALO, _mm(h_s[...], wqkv_ref[...]))
        yield

        def emit_qkv(r0, c0, blk):
            blk = _silu(blk)
            rows = slice(r0, r0 + CONV_ROWS)
            if c0 < 2 * QK_A:
                blk = blk * lax.rsqrt(jnp.sum(blk * blk, axis=-1, keepdims=True) + EPS)
            if c0 < QK_A:
                q_s[rows, c0:c0 + LANES] = blk * (GDN_DK ** -0.5)
            elif c0 < 2 * QK_A:
                k_s[rows, c0 - QK_A:c0 - QK_A + LANES] = blk
            else:
                v_s[rows, c0 - 2 * QK_A:c0 - 2 * QK_A + LANES] = blk

        for _ in _causal_dwconv(qkv_s, convw_ref, QKV_HALO, tm, GDN_CONV, emit_qkv):
            pass
        yield

        lanes = [slice(hh * GDN_DK, (hh + 1) * GDN_DK) for hh in heads]
        q_h = [q_s[:, ls] for ls in lanes]
        k_h = [k_s[:, ls] for ls in lanes]
        bcol = [beta[:, GDN_HEADS + hh:GDN_HEADS + hh + 1] for hh in heads]
        egcol = [eg[:, hh:hh + 1] for hh in heads]
        kb = [k_h[hh] * bcol[hh] for hh in heads]
        a_mat = [jnp.where(strict, _mm_nt(kb[hh], k_h[hh]) * decay[hh], 0.0) for hh in heads]
        yield
        qk = [jnp.where(causal, _mm_nt(q_h[hh], k_h[hh]) * decay[hh], 0.0) for hh in heads]
        t_mat = [eye - a_mat[hh] * lvl_ref[0] for hh in heads]
        yield
        for j in range(1, chunk_bits):
            half = 1 << j
            if half < SUBLANES:
                xt = [_mm(a_mat[hh] * lvl_ref[j], t_mat[hh]) for hh in heads]
                yield
                t_mat = [t_mat[hh] - _mm(t_mat[hh], xt[hh]) for hh in heads]
                yield
                continue
            lower = [slice(r, r + half) for r in range(half, tm, 2 * half)]
            upper = [slice(r, r + half) for r in range(0, tm, 2 * half)]
            pick = lambda m: jnp.concatenate([m[rows] for rows in lower], axis=0)
            sel = jnp.concatenate([lvl_ref[j, rows, :] for rows in lower], axis=0)
            xl = [_mm(pick(a_mat[hh]) * sel, t_mat[hh]) for hh in heads]
            yield
            zeros = jnp.zeros((half, tm), F32)
            upd = []
            for hh in heads:
                x_full = jnp.concatenate(
                    [blk for i in range(len(lower)) for blk in (zeros, xl[hh][i * half:(i + 1) * half])],
                    axis=0)
                upd.append(_mm(pick(t_mat[hh]), x_full))
            t_mat = [jnp.concatenate(
                [blk for i in range(len(lower))
                 for blk in (t_mat[hh][upper[i]], t_mat[hh][lower[i]] - upd[hh][i * half:(i + 1) * half])],
                axis=0) for hh in heads]
            yield
        uw = [_mm(t_mat[hh], jnp.concatenate([v_s[:, lanes[hh]] * bcol[hh], kb[hh] * egcol[hh]], axis=1))
              for hh in heads]
        yield
        loc = [_mm(qk[hh], uw[hh]) for hh in heads]
        k_dec = [k_h[hh] * egl[:, hh:hh + 1] for hh in heads]
        chunk_rows = [slice(c * GDN_CHUNK, (c + 1) * GDN_CHUNK) for c in range(n_chunks)]
        kuw = [[_mm_tn(k_dec[hh][rows], uw[hh][rows]) for rows in chunk_rows] for hh in heads]
        yield
        q_til = [q_h[hh] * egcol[hh] - loc[hh][:, GDN_DV:] for hh in heads]
        st = [state_s[hh] for hh in heads]
        o_parts = [[] for _ in heads]
        for c, rows in enumerate(chunk_rows):
            for hh in heads:
                r = _mm(jnp.concatenate([kuw[hh][c][:, GDN_DV:], q_til[hh][rows]], axis=0), st[hh])
                o_parts[hh].append(r[GDN_DK:])
                st[hh] = st[hh] * eg_last[c][:, hh:hh + 1] + kuw[hh][c][:, :GDN_DV] - r[:GDN_DK]
            yield
        z = _mm(h_s[...], wz_ref[...])
        for hh in heads:
            state_s[hh] = st[hh]
            o = jnp.concatenate(o_parts[hh], axis=0) + loc[hh][:, :GDN_DV]
            o_s[:, lanes[hh]] = _rmsnorm(o, gnorm_ref[...]) * _silu(z[:, lanes[hh]])
        yield
        gated(0, _mm(o_s[...], wgdn_ref[...]))

    def branch_b():
        glu = _mm(h_s[...], wglu_ref[...]) + glub_ref[...]
        _to_slabs(ubuf_s, CC_HALO, glu[:, :CONV_CH] * _sigmoid(glu[:, CONV_CH:]))
        yield

        def emit_cc(r0, c0, blk):
            conv_s[r0:r0 + CONV_ROWS, c0:c0 + LANES] = blk + dwb_ref[:, c0:c0 + LANES]

        yield from _causal_dwconv(ubuf_s, dww_ref, CC_HALO, tm, CONV_K, emit_cc)
        for r0 in range(0, tm, CONV_ROWS):
            blk = conv_s[r0:r0 + CONV_ROWS, :]
            mu = jnp.mean(blk, axis=-1, keepdims=True)
            cen = blk - mu
            var = jnp.mean(cen * cen, axis=-1, keepdims=True)
            conv_s[r0:r0 + CONV_ROWS, :] = _silu(cen * lax.rsqrt(var + EPS) * lnw_ref[...] + lnb_ref[...])
            yield
        gated(1, _mm(conv_s[...], wcc_ref[...]))

    def branch_c():
        qc = _mm(h_s[...], wqc_ref[...])
        yield
        for hh in range(XA_HEADS):
            ls = slice(hh * XA_DH, (hh + 1) * XA_DH)
            k_m = kvm_ref[0, :, hh * XA_DH:(hh + 1) * XA_DH]
            v_m = kvm_ref[0, :, XA_W + hh * XA_DH:XA_W + (hh + 1) * XA_DH]
            sc = _mm_nt(qc[:, ls], k_m) * (XA_DH ** -0.5)
            ex = jnp.exp(sc - jnp.max(sc, axis=-1, keepdims=True))
            att_s[:, ls] = _mm(ex, v_m) * (1.0 / jnp.sum(ex, axis=-1, keepdims=True))
            yield
        gated(2, _mm(att_s[...], wxa_ref[...]))

    _interleave(branch_a(), [gates(), branch_b(), branch_c()], FILL_RATIO)
    merged = y_s[0] + y_s[1] + y_s[2]
    out_ref[0] = x + _mm(merged, wo_ref[...])
    hn_s[...] = _rmsnorm(xn_ref[0], nmix_ref[...]).astype(BF16)


def _level_masks(tm):
    r = np.arange(tm)[:, None]
    c = np.arange(tm)[None, :]
    same_chunk = (r // GDN_CHUNK) == (c // GDN_CHUNK)
    n_levels = GDN_CHUNK.bit_length() - 1
    masks = [same_chunk & (r > c) & (((r ^ c) >> j) == 1) for j in range(n_levels)]
    return jnp.asarray(np.stack(masks).astype(np.float32))


def _layer_spec(arr, layer):
    nd = arr.ndim - 1
    return pl.BlockSpec((None,) + arr.shape[1:], lambda b, s: (layer,) + (0,) * nd,
                        pipeline_mode=pl.Buffered(1))


def _const_spec(arr):
    nd = arr.ndim
    return pl.BlockSpec(arr.shape, lambda b, s: (0,) * nd, pipeline_mode=pl.Buffered(1))


def _mixer_params(norm_mix, w_in, gdn_conv_w, gdn_dt_bias, gdn_a_log, gdn_norm, w_gdn_out,
                  cc_glu_b, cc_dw_w, cc_dw_b, cc_ln_w, cc_ln_b, w_cc_out, w_xa_out, gate_b, w_o):
    p0 = QKV_A
    p1 = p0 + GDN_HEADS
    p2 = p1 + GDN_HEADS
    p3 = p2 + V_A
    p4 = p3 + 2 * CONV_CH
    p5 = p4 + XA_W
    w_ab = jnp.pad(w_in[:, :, p0:p2], ((0, 0), (0, 0), (0, LANES - 2 * GDN_HEADS)))
    pad_row = lambda v: jnp.pad(v, ((0, 0), (0, LANES - v.shape[1])))[:, None, :]
    row = lambda v: v[:, None, :]
    return [
        row(norm_mix), w_in[:, :, :p0].astype(BF16), w_ab.astype(BF16), w_in[:, :, p2:p3].astype(BF16),
        w_in[:, :, p3:p4].astype(BF16), w_in[:, :, p4:p5].astype(BF16), w_in[:, :, p5:].astype(BF16),
        gdn_conv_w, pad_row(gdn_a_log), pad_row(gdn_dt_bias), row(gdn_norm), w_gdn_out.astype(BF16),
        row(cc_glu_b), cc_dw_w, row(cc_dw_b), row(cc_ln_w), row(cc_ln_b), w_cc_out.astype(BF16),
        w_xa_out.astype(BF16), row(gate_b), w_o.astype(BF16),
    ]


def _mixer(x, kvm, params, lvl, layer):
    bn, sn, d = x.shape
    tm = MIXER_ROWS
    mn, kvw = kvm.shape[2], kvm.shape[3]
    return pl.pallas_call(
        _mixer_kernel,
        grid=(bn, sn // tm),
        in_specs=[pl.BlockSpec((1, tm, d), lambda b, s: (b, s, 0)),
                  pl.BlockSpec((1, tm, d), lambda b, s: (b, jnp.minimum(s + 1, sn // tm - 1), 0)),
                  pl.BlockSpec((None, 1, mn, kvw), lambda b, s: (layer, b, 0, 0))]
                 + [_layer_spec(p, layer) for p in params] + [_const_spec(lvl)],
        out_specs=pl.BlockSpec((1, tm, d), lambda b, s: (b, s, 0)),
        out_shape=jax.ShapeDtypeStruct(x.shape, x.dtype),
        scratch_shapes=[
            pltpu.VMEM((tm, d), BF16),
            pltpu.VMEM((tm, d), BF16),
            pltpu.VMEM((QKV_A // LANES, QKV_HALO + tm, LANES), F32),
            pltpu.VMEM((tm, QK_A), F32),
            pltpu.VMEM((tm, QK_A), F32),
            pltpu.VMEM((tm, V_A), F32),
            pltpu.VMEM((tm, V_A), F32),
            pltpu.VMEM((CONV_CH // LANES, CC_HALO + tm, LANES), F32),
            pltpu.VMEM((tm, CONV_CH), F32),
            pltpu.VMEM((tm, XA_W), F32),
            pltpu.VMEM((GDN_HEADS, GDN_DK, GDN_DV), F32),
            pltpu.VMEM((tm, N_BRANCH * d), F32),
            pltpu.VMEM((N_BRANCH, tm, d), F32),
        ],
        compiler_params=pltpu.CompilerParams(
            dimension_semantics=("arbitrary", "arbitrary"),
            vmem_limit_bytes=VMEM_LIMIT_BYTES),
        name="mixer",
    )(x, x, kvm, *params, lvl)


def _ffn_kernel(x_ref, xn_ref, nffn_ref, wup_ref, dww_ref, dwb_ref, wdown_ref, nfin_ref, out_ref,
                h_s, hn_s, gbuf_s, act_s, *, final):
    tm = x_ref.shape[1]
    f = dww_ref.shape[1]

    @pl.when(pl.program_id(1) == 0)
    def _():
        gbuf_s[:, 0:FFN_HALO, :] = jnp.zeros((gbuf_s.shape[0], FFN_HALO, LANES), F32)
        h_s[...] = _rmsnorm(x_ref[0], nffn_ref[...]).astype(BF16)

    @pl.when(pl.program_id(1) != 0)
    def _():
        h_s[...] = hn_s[...]

    x = x_ref[0]
    _to_slabs(gbuf_s, FFN_HALO, _mm(h_s[...], wup_ref[:, :f]))
    act_s[...] = _mm(h_s[...], wup_ref[:, f:])

    def emit(r0, c0, blk):
        rows, cols = slice(r0, r0 + CONV_ROWS), slice(c0, c0 + LANES)
        act_s[rows, cols] = _silu(blk + dwb_ref[:, cols]) * act_s[rows, cols]

    for _ in _causal_dwconv(gbuf_s, dww_ref, FFN_HALO, tm, FFN_CONV, emit):
        pass
    y = x + _mm(act_s[...], wdown_ref[...])
    if final:
        y = _rmsnorm(y, nfin_ref[...])
    out_ref[0] = y
    hn_s[...] = _rmsnorm(xn_ref[0], nffn_ref[...]).astype(BF16)


def _ffn_params(norm_ffn, w_up, ffn_dw_w, ffn_dw_b, w_down):
    row = lambda v: v[:, None, :]
    return [row(norm_ffn), w_up.astype(BF16), ffn_dw_w, row(ffn_dw_b), w_down.astype(BF16)]


def _ffn(x, params, norm_final, layer, final):
    bn, sn, d = x.shape
    tm = FFN_ROWS
    f = params[4].shape[1]
    nfin = norm_final.reshape(1, d)
    return pl.pallas_call(
        functools.partial(_ffn_kernel, final=final),
        grid=(bn, sn // tm),
        in_specs=[pl.BlockSpec((1, tm, d), lambda b, s: (b, s, 0)),
                  pl.BlockSpec((1, tm, d), lambda b, s: (b, jnp.minimum(s + 1, sn // tm - 1), 0))]
                 + [_layer_spec(p, layer) for p in params] + [_const_spec(nfin)],
        out_specs=pl.BlockSpec((1, tm, d), lambda b, s: (b, s, 0)),
        out_shape=jax.ShapeDtypeStruct(x.shape, x.dtype),
        scratch_shapes=[
            pltpu.VMEM((tm, d), BF16),
            pltpu.VMEM((tm, d), BF16),
            pltpu.VMEM((f // LANES, FFN_HALO + tm, LANES), F32),
            pltpu.VMEM((tm, f), F32),
        ],
        compiler_params=pltpu.CompilerParams(
            dimension_semantics=("arbitrary", "arbitrary"),
            vmem_limit_bytes=VMEM_LIMIT_BYTES),
        name="ffn",
    )(x, x, *params, nfin)


def kernel(x, mem, norm_mix, w_in, gdn_conv_w, gdn_dt_bias, gdn_a_log, gdn_norm, w_gdn_out, cc_glu_b, cc_dw_w, cc_dw_b, cc_ln_w, cc_ln_b, w_cc_out, mem_norm, w_mem_kv, w_xa_out, gate_b, w_o, norm_ffn, w_up, ffn_dw_w, ffn_dw_b, w_down, norm_final):
    n_layers = w_in.shape[0]
    assert x.shape[1] % MIXER_ROWS == 0 and x.shape[1] % FFN_ROWS == 0
    assert MIXER_ROWS % GDN_CHUNK == 0 and MIXER_ROWS % CONV_ROWS == 0 and FFN_ROWS % CONV_ROWS == 0
    kvm = _memkv(mem, mem_norm, w_mem_kv)
    mixer_params = _mixer_params(norm_mix, w_in, gdn_conv_w, gdn_dt_bias, gdn_a_log, gdn_norm,
                                 w_gdn_out, cc_glu_b, cc_dw_w, cc_dw_b, cc_ln_w, cc_ln_b, w_cc_out,
                                 w_xa_out, gate_b, w_o)
    ffn_params = _ffn_params(norm_ffn, w_up, ffn_dw_w, ffn_dw_b, w_down)
    lvl = _level_masks(MIXER_ROWS)
    for l in range(n_layers):
        x = _mixer(x, kvm, mixer_params, lvl, l)
        x = _ffn(x, ffn_params, norm_final, l, final=(l == n_layers - 1))
    return x
```

```python
import functools

import numpy as np
import jax
import jax.numpy as jnp
from jax import lax
from jax.experimental import pallas as pl
from jax.experimental.pallas import tpu as pltpu

EPS = 1e-6
GDN_HEADS = 4
GDN_DK = 128
GDN_DV = 128
GDN_CONV = 4
GDN_CHUNK = 64
QK_A = GDN_HEADS * GDN_DK
V_A = GDN_HEADS * GDN_DV
QKV_A = 2 * QK_A + V_A
CONV_CH = 512
CONV_K = 31
XA_HEADS = 4
XA_DH = 128
XA_W = XA_HEADS * XA_DH
N_BRANCH = 3
FFN_CONV = 3

LANES = 128
SUBLANES = 8
MIXER_ROWS = 256
FFN_ROWS = 512
SPLIT_ROWS = 256
CONV_ROWS = 64
GATE_COLS = 256
FILL_RATIO = 2
QKV_HALO = SUBLANES
CC_HALO = 4 * SUBLANES
FFN_HALO = SUBLANES

BF16 = jnp.bfloat16
F32 = jnp.float32


def _nbytes(shape, dtype):
    return int(np.prod(shape)) * jnp.dtype(dtype).itemsize


def _vmem_limit(pipelined, resident, scratch, temp_bytes):
    return (2 * sum(_nbytes(*e) for e in pipelined) + sum(_nbytes(*e) for e in resident)
            + sum(_nbytes(*e) for e in scratch) + temp_bytes)


def _mm(a, b):
    return jnp.dot(a.astype(BF16), b.astype(BF16), preferred_element_type=F32)


def _mm_nt(a, b):
    return lax.dot_general(a.astype(BF16), b.astype(BF16), (((1,), (1,)), ((), ())),
                           preferred_element_type=F32)


def _mm_tn(a, b):
    return lax.dot_general(a.astype(BF16), b.astype(BF16), (((0,), (0,)), ((), ())),
                           preferred_element_type=F32)


def _rmsnorm(x, w_row):
    ms = jnp.mean(x * x, axis=-1, keepdims=True)
    return x * lax.rsqrt(ms + EPS) * w_row


def _sigmoid(x):
    return 0.5 * jnp.tanh(0.5 * x) + 0.5


def _silu(x):
    return x * _sigmoid(x)


def _softplus(x):
    return jnp.maximum(x, 0.0) + jnp.log(1.0 + jnp.exp(-jnp.abs(x)))


def _interleave(main, fillers, ratio):
    live = list(fillers)

    def fill(n):
        while n > 0 and live:
            for g in list(live):
                if n == 0:
                    break
                try:
                    next(g)
                    n -= 1
                except StopIteration:
                    live.remove(g)

    for _ in main:
        fill(ratio)
    fill(float("inf"))


def _to_slabs(buf_ref, halo, val):
    for cb in range(buf_ref.shape[0]):
        buf_ref[cb, halo:halo + val.shape[0], :] = val[:, cb * LANES:(cb + 1) * LANES]


def _causal_dwconv(buf_ref, w_ref, halo, n_rows, n_taps, emit):
    for cb in range(buf_ref.shape[0]):
        c0 = cb * LANES
        for r0 in range(0, n_rows, CONV_ROWS):
            acc = None
            for k in range(n_taps):
                start = halo - (n_taps - 1) + k + r0
                term = w_ref[k:k + 1, c0:c0 + LANES] * buf_ref[cb, start:start + CONV_ROWS, :]
                acc = term if acc is None else acc + term
            emit(r0, c0, acc)
            yield
    buf_ref[:, 0:halo, :] = buf_ref[:, n_rows:n_rows + halo, :]


def _memkv_kernel(mem_ref, nrm_ref, w_ref, out_ref):
    m = _rmsnorm(mem_ref[0], nrm_ref[0])
    out_ref[0, 0] = _mm(m, w_ref[0]).astype(out_ref.dtype)


def _memkv(mem, mem_norm, w_mem_kv):
    n_layers = w_mem_kv.shape[0]
    bn, mn, d = mem.shape
    n_out = w_mem_kv.shape[2]
    return pl.pallas_call(
        _memkv_kernel,
        grid=(n_layers, bn),
        in_specs=[
            pl.BlockSpec((1, mn, d), lambda l, b: (b, 0, 0)),
            pl.BlockSpec((1, 1, d), lambda l, b: (l, 0, 0)),
            pl.BlockSpec((1, d, n_out), lambda l, b: (l, 0, 0)),
        ],
        out_specs=pl.BlockSpec((1, 1, mn, n_out), lambda l, b: (l, b, 0, 0)),
        out_shape=jax.ShapeDtypeStruct((n_layers, bn, mn, n_out), BF16),
        compiler_params=pltpu.CompilerParams(
            dimension_semantics=("arbitrary", "arbitrary"),
            vmem_limit_bytes=_vmem_limit(
                pipelined=[((mn, d), mem.dtype), ((SUBLANES, d), F32), ((d, n_out), BF16), ((mn, n_out), BF16)],
                resident=[], scratch=[],
                temp_bytes=_nbytes((mn, d), F32) + _nbytes((mn, n_out), F32))),
        name="mem_kv",
    )(mem, mem_norm.reshape(n_layers, 1, d), w_mem_kv.astype(BF16))


def _mixer_kernel(x_ref, kvm_ref, nmix_ref, wqkv_ref, wab_ref, wz_ref, wglu_ref, wqc_ref,
                  wgate_ref, convw_ref, alog_ref, dtb_ref, gnorm_ref, wgdn_ref,
                  glub_ref, dww_ref, dwb_ref, lnw_ref, lnb_ref, wcc_ref,
                  wxa_ref, gateb_ref, wo_ref, lvl_ref,
                  out_ref,
                  h_s, qkv_s, q_s, k_s, v_s, o_s, ubuf_s, conv_s, att_s, state_s, gate_s, y_s):
    tm = x_ref.shape[1]
    d = x_ref.shape[2]

    @pl.when(pl.program_id(1) == 0)
    def _():
        qkv_s[:, 0:QKV_HALO, :] = jnp.zeros((qkv_s.shape[0], QKV_HALO, LANES), F32)
        ubuf_s[:, 0:CC_HALO, :] = jnp.zeros((ubuf_s.shape[0], CC_HALO, LANES), F32)
        state_s[...] = jnp.zeros(state_s.shape, F32)

    x = x_ref[0]
    h_s[...] = _rmsnorm(x, nmix_ref[...]).astype(BF16)

    gate_cols_written = set()

    def gates():
        for c0 in range(0, N_BRANCH * d, GATE_COLS):
            cols = slice(c0, c0 + GATE_COLS)
            gate_s[:, cols] = _sigmoid(_mm(h_s[...], wgate_ref[:, cols]) + gateb_ref[:, cols])
            gate_cols_written.add(c0)
            yield

    def gated(branch, y):
        assert all(c0 in gate_cols_written for c0 in range(branch * d, (branch + 1) * d, GATE_COLS))
        y_s[branch] = gate_s[:, branch * d:(branch + 1) * d] * y

    def branch_a():
        ab = _mm(h_s[...], wab_ref[...])
        g = -jnp.exp(alog_ref[...]) * _softplus(ab + dtb_ref[...])
        beta = _sigmoid(ab)

        n_chunks = tm // GDN_CHUNK
        chunk_bits = GDN_CHUNK.bit_length() - 1
        ri = lax.broadcasted_iota(jnp.int32, (tm, tm), 0)
        ci = lax.broadcasted_iota(jnp.int32, (tm, tm), 1)
        same_chunk = (ri >> chunk_bits) == (ci >> chunk_bits)
        causal = same_chunk & (ri >= ci)
        strict = same_chunk & (ri > ci)
        eye = (ri == ci).astype(F32)

        tri = causal.astype(BF16)
        g_hi = g.astype(BF16)
        g_r1 = g - g_hi.astype(F32)
        g_mid = g_r1.astype(BF16)
        g_lo = (g_r1 - g_mid.astype(F32)).astype(BF16)
        gc = _mm(tri, g_hi) + _mm(tri, g_mid) + _mm(tri, g_lo)
        gct = gc.T
        eg = jnp.exp(gc)
        egl_parts, eg_last = [], []
        for c in range(n_chunks):
            gl = gc[(c + 1) * GDN_CHUNK - 1:(c + 1) * GDN_CHUNK, :]
            egl_parts.append(jnp.exp(gl - gc[c * GDN_CHUNK:(c + 1) * GDN_CHUNK, :]))
            eg_last.append(jnp.exp(gl))
        egl = jnp.concatenate(egl_parts, axis=0)
        heads = range(GDN_HEADS)
        decay = [jnp.exp(jnp.where(causal, gc[:, hh:hh + 1] - gct[hh:hh + 1, :], -jnp.inf))
                 for hh in heads]
        yield

        _to_slabs(qkv_s, QKV_HALO, _mm(h_s[...], wqkv_ref[...]))
        yield

        def emit_qkv(r0, c0, blk):
            blk = _silu(blk)
            rows = slice(r0, r0 + CONV_ROWS)
            if c0 < 2 * QK_A:
                blk = blk * lax.rsqrt(jnp.sum(blk * blk, axis=-1, keepdims=True) + EPS)
            if c0 < QK_A:
                q_s[rows, c0:c0 + LANES] = blk * (GDN_DK ** -0.5)
            elif c0 < 2 * QK_A:
                k_s[rows, c0 - QK_A:c0 - QK_A + LANES] = blk
            else:
                v_s[rows, c0 - 2 * QK_A:c0 - 2 * QK_A + LANES] = blk

        for _ in _causal_dwconv(qkv_s, convw_ref, QKV_HALO, tm, GDN_CONV, emit_qkv):
            pass
        yield

        lanes = [slice(hh * GDN_DK, (hh + 1) * GDN_DK) for hh in heads]
        q_h = [q_s[:, ls] for ls in lanes]
        k_h = [k_s[:, ls] for ls in lanes]
        bcol = [beta[:, GDN_HEADS + hh:GDN_HEADS + hh + 1] for hh in heads]
        egcol = [eg[:, hh:hh + 1] for hh in heads]
        kb = [k_h[hh] * bcol[hh] for hh in heads]
        a_mat = [jnp.where(strict, _mm_nt(kb[hh], k_h[hh]) * decay[hh], 0.0) for hh in heads]
        yield
        qk = [jnp.where(causal, _mm_nt(q_h[hh], k_h[hh]) * decay[hh], 0.0) for hh in heads]
        t_mat = [eye - a_mat[hh] * lvl_ref[0] for hh in heads]
        yield
        for j in range(1, chunk_bits):
            half = 1 << j
            if half < SUBLANES:
                xt = [_mm(a_mat[hh] * lvl_ref[j], t_mat[hh]) for hh in heads]
                yield
                t_mat = [t_mat[hh] - _mm(t_mat[hh], xt[hh]) for hh in heads]
                yield
                continue
            lower = [slice(r, r + half) for r in range(half, tm, 2 * half)]
            upper = [slice(r, r + half) for r in range(0, tm, 2 * half)]
            pick = lambda m: jnp.concatenate([m[rows] for rows in lower], axis=0)
            sel = jnp.concatenate([lvl_ref[j, rows, :] for rows in lower], axis=0)
            xl = [_mm(pick(a_mat[hh]) * sel, t_mat[hh]) for hh in heads]
            yield
            zeros = jnp.zeros((half, tm), F32)
            upd = []
            for hh in heads:
                x_full = jnp.concatenate(
                    [blk for i in range(len(lower)) for blk in (zeros, xl[hh][i * half:(i + 1) * half])],
                    axis=0)
                upd.append(_mm(pick(t_mat[hh]), x_full))
            t_mat = [jnp.concatenate(
                [blk for i in range(len(lower))
                 for blk in (t_mat[hh][upper[i]], t_mat[hh][lower[i]] - upd[hh][i * half:(i + 1) * half])],
                axis=0) for hh in heads]
            yield
        uw = [_mm(t_mat[hh], jnp.concatenate([v_s[:, lanes[hh]] * bcol[hh], kb[hh] * egcol[hh]], axis=1))
              for hh in heads]
        yield
        loc = [_mm(qk[hh], uw[hh]) for hh in heads]
        k_dec = [k_h[hh] * egl[:, hh:hh + 1] for hh in heads]
        chunk_rows = [slice(c * GDN_CHUNK, (c + 1) * GDN_CHUNK) for c in range(n_chunks)]
        kuw = [[_mm_tn(k_dec[hh][rows], uw[hh][rows]) for rows in chunk_rows] for hh in heads]
        yield
        q_til = [q_h[hh] * egcol[hh] - loc[hh][:, GDN_DV:] for hh in heads]
        st = [state_s[hh] for hh in heads]
        o_parts = [[] for _ in heads]
        for c, rows in enumerate(chunk_rows):
            for hh in heads:
                r = _mm(jnp.concatenate([kuw[hh][c][:, GDN_DV:], q_til[hh][rows]], axis=0), st[hh])
                o_parts[hh].append(r[GDN_DK:])
                st[hh] = st[hh] * eg_last[c][:, hh:hh + 1] + kuw[hh][c][:, :GDN_DV] - r[:GDN_DK]
            yield
        z = _mm(h_s[...], wz_ref[...])
        for hh in heads:
            state_s[hh] = st[hh]
            o = jnp.concatenate(o_parts[hh], axis=0) + loc[hh][:, :GDN_DV]
            o_s[:, lanes[hh]] = _rmsnorm(o, gnorm_ref[...]) * _silu(z[:, lanes[hh]])
        yield
        gated(0, _mm(o_s[...], wgdn_ref[...]))

    def branch_b():
        glu = _mm(h_s[...], wglu_ref[...]) + glub_ref[...]
        _to_slabs(ubuf_s, CC_HALO, glu[:, :CONV_CH] * _sigmoid(glu[:, CONV_CH:]))
        yield

        def emit_cc(r0, c0, blk):
            conv_s[r0:r0 + CONV_ROWS, c0:c0 + LANES] = blk + dwb_ref[:, c0:c0 + LANES]

        yield from _causal_dwconv(ubuf_s, dww_ref, CC_HALO, tm, CONV_K, emit_cc)
        for r0 in range(0, tm, CONV_ROWS):
            blk = conv_s[r0:r0 + CONV_ROWS, :]
            mu = jnp.mean(blk, axis=-1, keepdims=True)
            cen = blk - mu
            var = jnp.mean(cen * cen, axis=-1, keepdims=True)
            conv_s[r0:r0 + CONV_ROWS, :] = _silu(cen * lax.rsqrt(var + EPS) * lnw_ref[...] + lnb_ref[...])
            yield
        gated(1, _mm(conv_s[...], wcc_ref[...]))

    def branch_c():
        qc = _mm(h_s[...], wqc_ref[...])
        yield
        for hh in range(XA_HEADS):
            ls = slice(hh * XA_DH, (hh + 1) * XA_DH)
            k_m = kvm_ref[0, :, hh * XA_DH:(hh + 1) * XA_DH]
            v_m = kvm_ref[0, :, XA_W + hh * XA_DH:XA_W + (hh + 1) * XA_DH]
            sc = _mm_nt(qc[:, ls], k_m) * (XA_DH ** -0.5)
            ex = jnp.exp(sc - jnp.max(sc, axis=-1, keepdims=True))
            att_s[:, ls] = _mm(ex, v_m) * (1.0 / jnp.sum(ex, axis=-1, keepdims=True))
            yield
        gated(2, _mm(att_s[...], wxa_ref[...]))

    _interleave(branch_a(), [gates(), branch_b(), branch_c()], FILL_RATIO)
    merged = y_s[0] + y_s[1] + y_s[2]
    out_ref[0] = x + _mm(merged, wo_ref[...])


def _level_masks(tm):
    r = np.arange(tm)[:, None]
    c = np.arange(tm)[None, :]
    same_chunk = (r // GDN_CHUNK) == (c // GDN_CHUNK)
    n_levels = GDN_CHUNK.bit_length() - 1
    masks = [same_chunk & (r > c) & (((r ^ c) >> j) == 1) for j in range(n_levels)]
    return jnp.asarray(np.stack(masks).astype(np.float32))


def _layer_spec(arr, layer):
    nd = arr.ndim - 1
    return pl.BlockSpec((None,) + arr.shape[1:], lambda b, s: (layer,) + (0,) * nd,
                        pipeline_mode=pl.Buffered(1))


def _const_spec(arr):
    nd = arr.ndim
    return pl.BlockSpec(arr.shape, lambda b, s: (0,) * nd, pipeline_mode=pl.Buffered(1))


def _w_in_splits(d_model):
    widths = (QKV_A, 2 * GDN_HEADS, V_A, 2 * CONV_CH, XA_W, N_BRANCH * d_model)
    edges = np.cumsum((0,) + widths)
    return tuple((int(c0), int(c1)) for c0, c1 in zip(edges[:-1], edges[1:]))


def _split_w_in_kernel(w_ref, *out_refs, splits):
    w = w_ref[0]
    for (c0, c1), o_ref in zip(splits, out_refs):
        piece = w[:, c0:c1]
        pad = o_ref.shape[2] - (c1 - c0)
        if pad:
            piece = jnp.concatenate([piece, jnp.zeros((piece.shape[0], pad), piece.dtype)], axis=1)
        o_ref[0] = piece.astype(o_ref.dtype)


def _split_w_in(w_in):
    n_layers, k, n = w_in.shape
    splits = _w_in_splits(k)
    assert splits[-1][1] == n and k % SPLIT_ROWS == 0
    widths = [-(-(c1 - c0) // LANES) * LANES for c0, c1 in splits]
    in_block = SPLIT_ROWS * n * w_in.dtype.itemsize
    out_block = SPLIT_ROWS * sum(widths) * jnp.dtype(BF16).itemsize
    return pl.pallas_call(
        functools.partial(_split_w_in_kernel, splits=splits),
        grid=(n_layers, k // SPLIT_ROWS),
        in_specs=[pl.BlockSpec((1, SPLIT_ROWS, n), lambda l, r: (l, r, 0))],
        out_specs=[pl.BlockSpec((1, SPLIT_ROWS, wd), lambda l, r: (l, r, 0)) for wd in widths],
        out_shape=[jax.ShapeDtypeStruct((n_layers, k, wd), BF16) for wd in widths],
        compiler_params=pltpu.CompilerParams(
            dimension_semantics=("arbitrary", "arbitrary"),
            vmem_limit_bytes=2 * (in_block + out_block) + 2 * in_block),
        name="split_w_in",
    )(w_in)


def _mixer_params(norm_mix, w_in, gdn_conv_w, gdn_dt_bias, gdn_a_log, gdn_norm, w_gdn_out,
                  cc_glu_b, cc_dw_w, cc_dw_b, cc_ln_w, cc_ln_b, w_cc_out, w_xa_out, gate_b, w_o):
    w_qkv, w_ab, w_z, w_glu, w_qc, w_gate = _split_w_in(w_in)
    pad_row = lambda v: jnp.pad(v, ((0, 0), (0, LANES - v.shape[1])))[:, None, :]
    row = lambda v: v[:, None, :]
    return [
        row(norm_mix), w_qkv, w_ab, w_z, w_glu, w_qc, w_gate,
        gdn_conv_w, pad_row(gdn_a_log), pad_row(gdn_dt_bias), row(gdn_norm), w_gdn_out.astype(BF16),
        row(cc_glu_b), cc_dw_w, row(cc_dw_b), row(cc_ln_w), row(cc_ln_b), w_cc_out.astype(BF16),
        w_xa_out.astype(BF16), row(gate_b), w_o.astype(BF16),
    ]


def _mixer(x, kvm, params, lvl, layer):
    bn, sn, d = x.shape
    tm = MIXER_ROWS
    mn, kvw = kvm.shape[2], kvm.shape[3]
    scratch = [
        ((tm, d), BF16),
        ((QKV_A // LANES, QKV_HALO + tm, LANES), F32),
        ((tm, QK_A), F32),
        ((tm, QK_A), F32),
        ((tm, V_A), F32),
        ((tm, V_A), F32),
        ((CONV_CH // LANES, CC_HALO + tm, LANES), F32),
        ((tm, CONV_CH), F32),
        ((tm, XA_W), F32),
        ((GDN_HEADS, GDN_DK, GDN_DV), F32),
        ((tm, N_BRANCH * d), F32),
        ((N_BRANCH, tm, d), F32),
    ]
    return pl.pallas_call(
        _mixer_kernel,
        grid=(bn, sn // tm),
        in_specs=[pl.BlockSpec((1, tm, d), lambda b, s: (b, s, 0)),
                  pl.BlockSpec((None, 1, mn, kvw), lambda b, s: (layer, b, 0, 0))]
                 + [_layer_spec(p, layer) for p in params] + [_const_spec(lvl)],
        out_specs=pl.BlockSpec((1, tm, d), lambda b, s: (b, s, 0)),
        out_shape=jax.ShapeDtypeStruct(x.shape, x.dtype),
        scratch_shapes=[pltpu.VMEM(shape, dtype) for shape, dtype in scratch],
        compiler_params=pltpu.CompilerParams(
            dimension_semantics=("arbitrary", "arbitrary"),
            vmem_limit_bytes=_vmem_limit(
                pipelined=[((tm, d), x.dtype), ((tm, d), x.dtype), ((mn, kvw), kvm.dtype)],
                resident=[(p.shape[1:], p.dtype) for p in params] + [(lvl.shape, lvl.dtype)],
                scratch=scratch,
                temp_bytes=2 * _nbytes((tm, QKV_A), F32) + 6 * GDN_HEADS * _nbytes((tm, tm), F32))),
        name="mixer",
    )(x, kvm, *params, lvl)


def _ffn_kernel(x_ref, nffn_ref, wup_ref, dww_ref, dwb_ref, wdown_ref, nfin_ref, out_ref,
                gbuf_s, act_s, *, final):
    tm = x_ref.shape[1]
    f = dww_ref.shape[1]

    @pl.when(pl.program_id(1) == 0)
    def _():
        gbuf_s[:, 0:FFN_HALO, :] = jnp.zeros((gbuf_s.shape[0], FFN_HALO, LANES), F32)

    x = x_ref[0]
    hb = _rmsnorm(x, nffn_ref[...]).astype(BF16)
    _to_slabs(gbuf_s, FFN_HALO, _mm(hb, wup_ref[:, :f]))
    act_s[...] = _mm(hb, wup_ref[:, f:])

    def emit(r0, c0, blk):
        rows, cols = slice(r0, r0 + CONV_ROWS), slice(c0, c0 + LANES)
        act_s[rows, cols] = _silu(blk + dwb_ref[:, cols]) * act_s[rows, cols]

    for _ in _causal_dwconv(gbuf_s, dww_ref, FFN_HALO, tm, FFN_CONV, emit):
        pass
    y = x + _mm(act_s[...], wdown_ref[...])
    if final:
        y = _rmsnorm(y, nfin_ref[...])
    out_ref[0] = y


def _ffn_params(norm_ffn, w_up, ffn_dw_w, ffn_dw_b, w_down):
    row = lambda v: v[:, None, :]
    return [row(norm_ffn), w_up.astype(BF16), ffn_dw_w, row(ffn_dw_b), w_down.astype(BF16)]


def _ffn(x, params, norm_final, layer, final):
    bn, sn, d = x.shape
    tm = FFN_ROWS
    f = params[4].shape[1]
    nfin = norm_final.reshape(1, d)
    scratch = [
        ((f // LANES, FFN_HALO + tm, LANES), F32),
        ((tm, f), F32),
    ]
    return pl.pallas_call(
        functools.partial(_ffn_kernel, final=final),
        grid=(bn, sn // tm),
        in_specs=[pl.BlockSpec((1, tm, d), lambda b, s: (b, s, 0))]
                 + [_layer_spec(p, layer) for p in params] + [_const_spec(nfin)],
        out_specs=pl.BlockSpec((1, tm, d), lambda b, s: (b, s, 0)),
        out_shape=jax.ShapeDtypeStruct(x.shape, x.dtype),
        scratch_shapes=[pltpu.VMEM(shape, dtype) for shape, dtype in scratch],
        compiler_params=pltpu.CompilerParams(
            dimension_semantics=("arbitrary", "arbitrary"),
            vmem_limit_bytes=_vmem_limit(
                pipelined=[((tm, d), x.dtype), ((tm, d), x.dtype)],
                resident=[(p.shape[1:], p.dtype) for p in params] + [(nfin.shape, nfin.dtype)],
                scratch=scratch,
                temp_bytes=_nbytes((tm, f), F32))),
        name="ffn",
    )(x, *params, nfin)


def kernel(x, mem, norm_mix, w_in, gdn_conv_w, gdn_dt_bias, gdn_a_log, gdn_norm, w_gdn_out, cc_glu_b, cc_dw_w, cc_dw_b, cc_ln_w, cc_ln_b, w_cc_out, mem_norm, w_mem_kv, w_xa_out, gate_b, w_o, norm_ffn, w_up, ffn_dw_w, ffn_dw_b, w_down, norm_final):
    n_layers = w_in.shape[0]
    assert x.shape[1] % MIXER_ROWS == 0 and x.shape[1] % FFN_ROWS == 0
    assert MIXER_ROWS % GDN_CHUNK == 0 and MIXER_ROWS % CONV_ROWS == 0 and FFN_ROWS % CONV_ROWS == 0
    kvm = _memkv(mem, mem_norm, w_mem_kv)
    mixer_params = _mixer_params(norm_mix, w_in, gdn_conv_w, gdn_dt_bias, gdn_a_log, gdn_norm,
                                 w_gdn_out, cc_glu_b, cc_dw_w, cc_dw_b, cc_ln_w, cc_ln_b, w_cc_out,
                                 w_xa_out, gate_b, w_o)
    ffn_params = _ffn_params(norm_ffn, w_up, ffn_dw_w, ffn_dw_b, w_down)
    lvl = _level_masks(MIXER_ROWS)
    for l in range(n_layers):
        x = _mixer(x, kvm, mixer_params, lvl, l)
        x = _ffn(x, ffn_params, norm_final, l, final=(l == n_layers - 1))
    return x
```

```python
import functools

import numpy as np
import jax
import jax.numpy as jnp
from jax import lax
from jax.experimental import pallas as pl
from jax.experimental.pallas import tpu as pltpu

EPS = 1e-6
GDN_HEADS = 4
GDN_DK = 128
GDN_DV = 128
GDN_CONV = 4
GDN_CHUNK = 64
QK_A = GDN_HEADS * GDN_DK
V_A = GDN_HEADS * GDN_DV
QKV_A = 2 * QK_A + V_A
CONV_CH = 512
CONV_K = 31
XA_HEADS = 4
XA_DH = 128
XA_W = XA_HEADS * XA_DH
N_BRANCH = 3
FFN_CONV = 3

LANES = 128
SUBLANES = 8
MIXER_ROWS = 256
FFN_ROWS = 512
SPLIT_COLS = 256
CONV_ROWS = 64
GATE_COLS = 256
FILL_RATIO = 2
QKV_HALO = SUBLANES
CC_HALO = 4 * SUBLANES
FFN_HALO = SUBLANES

BF16 = jnp.bfloat16
F32 = jnp.float32


def _nbytes(shape, dtype):
    return int(np.prod(shape)) * jnp.dtype(dtype).itemsize


def _vmem_limit(pipelined, resident, scratch, temp_bytes):
    return (2 * sum(_nbytes(*e) for e in pipelined) + sum(_nbytes(*e) for e in resident)
            + sum(_nbytes(*e) for e in scratch) + temp_bytes)


def _mm(a, b):
    return jnp.dot(a.astype(BF16), b.astype(BF16), preferred_element_type=F32)


def _mm_nt(a, b):
    return lax.dot_general(a.astype(BF16), b.astype(BF16), (((1,), (1,)), ((), ())),
                           preferred_element_type=F32)


def _mm_tn(a, b):
    return lax.dot_general(a.astype(BF16), b.astype(BF16), (((0,), (0,)), ((), ())),
                           preferred_element_type=F32)


def _rmsnorm(x, w_row):
    ms = jnp.mean(x * x, axis=-1, keepdims=True)
    return x * lax.rsqrt(ms + EPS) * w_row


def _sigmoid(x):
    return 0.5 * jnp.tanh(0.5 * x) + 0.5


def _silu(x):
    return x * _sigmoid(x)


def _softplus(x):
    return jnp.maximum(x, 0.0) + jnp.log(1.0 + jnp.exp(-jnp.abs(x)))


def _interleave(main, fillers, ratio):
    live = list(fillers)

    def fill(n):
        while n > 0 and live:
            for g in list(live):
                if n == 0:
                    break
                try:
                    next(g)
                    n -= 1
                except StopIteration:
                    live.remove(g)

    for _ in main:
        fill(ratio)
    fill(float("inf"))


def _to_slabs(buf_ref, halo, val):
    for cb in range(buf_ref.shape[0]):
        buf_ref[cb, halo:halo + val.shape[0], :] = val[:, cb * LANES:(cb + 1) * LANES]


def _causal_dwconv(buf_ref, w_ref, halo, n_rows, n_taps, emit):
    for cb in range(buf_ref.shape[0]):
        c0 = cb * LANES
        for r0 in range(0, n_rows, CONV_ROWS):
            acc = None
            for k in range(n_taps):
                start = halo - (n_taps - 1) + k + r0
                term = w_ref[k:k + 1, c0:c0 + LANES] * buf_ref[cb, start:start + CONV_ROWS, :]
                acc = term if acc is None else acc + term
            emit(r0, c0, acc)
            yield
    buf_ref[:, 0:halo, :] = buf_ref[:, n_rows:n_rows + halo, :]


def _memkv_kernel(mem_ref, nrm_ref, w_ref, out_ref):
    m = _rmsnorm(mem_ref[0], nrm_ref[0])
    out_ref[0, 0] = _mm(m, w_ref[0]).astype(out_ref.dtype)


def _memkv(mem, mem_norm, w_mem_kv):
    n_layers = w_mem_kv.shape[0]
    bn, mn, d = mem.shape
    n_out = w_mem_kv.shape[2]
    return pl.pallas_call(
        _memkv_kernel,
        grid=(n_layers, bn),
        in_specs=[
            pl.BlockSpec((1, mn, d), lambda l, b: (b, 0, 0)),
            pl.BlockSpec((1, 1, d), lambda l, b: (l, 0, 0)),
            pl.BlockSpec((1, d, n_out), lambda l, b: (l, 0, 0)),
        ],
        out_specs=pl.BlockSpec((1, 1, mn, n_out), lambda l, b: (l, b, 0, 0)),
        out_shape=jax.ShapeDtypeStruct((n_layers, bn, mn, n_out), BF16),
        compiler_params=pltpu.CompilerParams(
            dimension_semantics=("arbitrary", "arbitrary"),
            vmem_limit_bytes=_vmem_limit(
                pipelined=[((mn, d), mem.dtype), ((SUBLANES, d), F32), ((d, n_out), BF16), ((mn, n_out), BF16)],
                resident=[], scratch=[],
                temp_bytes=_nbytes((mn, d), F32) + _nbytes((mn, n_out), F32))),
        name="mem_kv",
    )(mem, mem_norm.reshape(n_layers, 1, d), w_mem_kv.astype(BF16))


def _mixer_kernel(x_ref, kvm_ref, nmix_ref, wqkv_ref, wab_ref, wz_ref, wglu_ref, wqc_ref,
                  wgate_ref, convw_ref, alog_ref, dtb_ref, gnorm_ref, wgdn_ref,
                  glub_ref, dww_ref, dwb_ref, lnw_ref, lnb_ref, wcc_ref,
                  wxa_ref, gateb_ref, wo_ref, lvl_ref,
                  out_ref,
                  h_s, qkv_s, q_s, k_s, v_s, o_s, ubuf_s, conv_s, att_s, state_s, gate_s, y_s):
    tm = x_ref.shape[1]
    d = x_ref.shape[2]

    @pl.when(pl.program_id(1) == 0)
    def _():
        qkv_s[:, 0:QKV_HALO, :] = jnp.zeros((qkv_s.shape[0], QKV_HALO, LANES), F32)
        ubuf_s[:, 0:CC_HALO, :] = jnp.zeros((ubuf_s.shape[0], CC_HALO, LANES), F32)
        state_s[...] = jnp.zeros(state_s.shape, F32)

    x = x_ref[0]
    h_s[...] = _rmsnorm(x, nmix_ref[...]).astype(BF16)

    gate_cols_written = set()

    def gates():
        for c0 in range(0, N_BRANCH * d, GATE_COLS):
            cols = slice(c0, c0 + GATE_COLS)
            gate_s[:, cols] = _sigmoid(_mm(h_s[...], wgate_ref[:, cols]) + gateb_ref[:, cols])
            gate_cols_written.add(c0)
            yield

    def gated(branch, y):
        assert all(c0 in gate_cols_written for c0 in range(branch * d, (branch + 1) * d, GATE_COLS))
        y_s[branch] = gate_s[:, branch * d:(branch + 1) * d] * y

    def branch_a():
        ab = _mm(h_s[...], wab_ref[...])
        g = -jnp.exp(alog_ref[...]) * _softplus(ab + dtb_ref[...])
        beta = _sigmoid(ab)

        n_chunks = tm // GDN_CHUNK
        chunk_bits = GDN_CHUNK.bit_length() - 1
        ri = lax.broadcasted_iota(jnp.int32, (tm, tm), 0)
        ci = lax.broadcasted_iota(jnp.int32, (tm, tm), 1)
        same_chunk = (ri >> chunk_bits) == (ci >> chunk_bits)
        causal = same_chunk & (ri >= ci)
        strict = same_chunk & (ri > ci)
        eye = (ri == ci).astype(F32)

        tri = causal.astype(BF16)
        g_hi = g.astype(BF16)
        g_r1 = g - g_hi.astype(F32)
        g_mid = g_r1.astype(BF16)
        g_lo = (g_r1 - g_mid.astype(F32)).astype(BF16)
        gc = _mm(tri, g_hi) + _mm(tri, g_mid) + _mm(tri, g_lo)
        gct = gc.T
        eg = jnp.exp(gc)
        egl_parts, eg_last = [], []
        for c in range(n_chunks):
            gl = gc[(c + 1) * GDN_CHUNK - 1:(c + 1) * GDN_CHUNK, :]
            egl_parts.append(jnp.exp(gl - gc[c * GDN_CHUNK:(c + 1) * GDN_CHUNK, :]))
            eg_last.append(jnp.exp(gl))
        egl = jnp.concatenate(egl_parts, axis=0)
        heads = range(GDN_HEADS)
        decay = [jnp.exp(jnp.where(causal, gc[:, hh:hh + 1] - gct[hh:hh + 1, :], -jnp.inf))
                 for hh in heads]
        yield

        _to_slabs(qkv_s, QKV_HALO, _mm(h_s[...], wqkv_ref[...]))
        yield

        def emit_qkv(r0, c0, blk):
            blk = _silu(blk)
            rows = slice(r0, r0 + CONV_ROWS)
            if c0 < 2 * QK_A:
                blk = blk * lax.rsqrt(jnp.sum(blk * blk, axis=-1, keepdims=True) + EPS)
            if c0 < QK_A:
                q_s[rows, c0:c0 + LANES] = blk * (GDN_DK ** -0.5)
            elif c0 < 2 * QK_A:
                k_s[rows, c0 - QK_A:c0 - QK_A + LANES] = blk
            else:
                v_s[rows, c0 - 2 * QK_A:c0 - 2 * QK_A + LANES] = blk

        for _ in _causal_dwconv(qkv_s, convw_ref, QKV_HALO, tm, GDN_CONV, emit_qkv):
            pass
        yield

        lanes = [slice(hh * GDN_DK, (hh + 1) * GDN_DK) for hh in heads]
        q_h = [q_s[:, ls] for ls in lanes]
        k_h = [k_s[:, ls] for ls in lanes]
        bcol = [beta[:, GDN_HEADS + hh:GDN_HEADS + hh + 1] for hh in heads]
        egcol = [eg[:, hh:hh + 1] for hh in heads]
        kb = [k_h[hh] * bcol[hh] for hh in heads]
        a_mat = [jnp.where(strict, _mm_nt(kb[hh], k_h[hh]) * decay[hh], 0.0) for hh in heads]
        yield
        qk = [jnp.where(causal, _mm_nt(q_h[hh], k_h[hh]) * decay[hh], 0.0) for hh in heads]
        t_mat = [eye - a_mat[hh] * lvl_ref[0] for hh in heads]
        yield
        for j in range(1, chunk_bits):
            half = 1 << j
            if half < SUBLANES:
                xt = [_mm(a_mat[hh] * lvl_ref[j], t_mat[hh]) for hh in heads]
                yield
                t_mat = [t_mat[hh] - _mm(t_mat[hh], xt[hh]) for hh in heads]
                yield
                continue
            lower = [slice(r, r + half) for r in range(half, tm, 2 * half)]
            upper = [slice(r, r + half) for r in range(0, tm, 2 * half)]
            pick = lambda m: jnp.concatenate([m[rows] for rows in lower], axis=0)
            sel = jnp.concatenate([lvl_ref[j, rows, :] for rows in lower], axis=0)
            xl = [_mm(pick(a_mat[hh]) * sel, t_mat[hh]) for hh in heads]
            yield
            zeros = jnp.zeros((half, tm), F32)
            upd = []
            for hh in heads:
                x_full = jnp.concatenate(
                    [blk for i in range(len(lower)) for blk in (zeros, xl[hh][i * half:(i + 1) * half])],
                    axis=0)
                upd.append(_mm(pick(t_mat[hh]), x_full))
            t_mat = [jnp.concatenate(
                [blk for i in range(len(lower))
                 for blk in (t_mat[hh][upper[i]], t_mat[hh][lower[i]] - upd[hh][i * half:(i + 1) * half])],
                axis=0) for hh in heads]
            yield
        uw = [_mm(t_mat[hh], jnp.concatenate([v_s[:, lanes[hh]] * bcol[hh], kb[hh] * egcol[hh]], axis=1))
              for hh in heads]
        yield
        loc = [_mm(qk[hh], uw[hh]) for hh in heads]
        k_dec = [k_h[hh] * egl[:, hh:hh + 1] for hh in heads]
        chunk_rows = [slice(c * GDN_CHUNK, (c + 1) * GDN_CHUNK) for c in range(n_chunks)]
        kuw = [[_mm_tn(k_dec[hh][rows], uw[hh][rows]) for rows in chunk_rows] for hh in heads]
        yield
        q_til = [q_h[hh] * egcol[hh] - loc[hh][:, GDN_DV:] for hh in heads]
        st = [state_s[hh] for hh in heads]
        o_parts = [[] for _ in heads]
        for c, rows in enumerate(chunk_rows):
            for hh in heads:
                r = _mm(jnp.concatenate([kuw[hh][c][:, GDN_DV:], q_til[hh][rows]], axis=0), st[hh])
                o_parts[hh].append(r[GDN_DK:])
                st[hh] = st[hh] * eg_last[c][:, hh:hh + 1] + kuw[hh][c][:, :GDN_DV] - r[:GDN_DK]
            yield
        z = _mm(h_s[...], wz_ref[...])
        for hh in heads:
            state_s[hh] = st[hh]
            o = jnp.concatenate(o_parts[hh], axis=0) + loc[hh][:, :GDN_DV]
            o_s[:, lanes[hh]] = _rmsnorm(o, gnorm_ref[...]) * _silu(z[:, lanes[hh]])
        yield
        gated(0, _mm(o_s[...], wgdn_ref[...]))

    def branch_b():
        glu = _mm(h_s[...], wglu_ref[...]) + glub_ref[...]
        _to_slabs(ubuf_s, CC_HALO, glu[:, :CONV_CH] * _sigmoid(glu[:, CONV_CH:]))
        yield

        def emit_cc(r0, c0, blk):
            conv_s[r0:r0 + CONV_ROWS, c0:c0 + LANES] = blk + dwb_ref[:, c0:c0 + LANES]

        yield from _causal_dwconv(ubuf_s, dww_ref, CC_HALO, tm, CONV_K, emit_cc)
        for r0 in range(0, tm, CONV_ROWS):
            blk = conv_s[r0:r0 + CONV_ROWS, :]
            mu = jnp.mean(blk, axis=-1, keepdims=True)
            cen = blk - mu
            var = jnp.mean(cen * cen, axis=-1, keepdims=True)
            conv_s[r0:r0 + CONV_ROWS, :] = _silu(cen * lax.rsqrt(var + EPS) * lnw_ref[...] + lnb_ref[...])
            yield
        gated(1, _mm(conv_s[...], wcc_ref[...]))

    def branch_c():
        qc = _mm(h_s[...], wqc_ref[...])
        yield
        for hh in range(XA_HEADS):
            ls = slice(hh * XA_DH, (hh + 1) * XA_DH)
            k_m = kvm_ref[0, :, hh * XA_DH:(hh + 1) * XA_DH]
            v_m = kvm_ref[0, :, XA_W + hh * XA_DH:XA_W + (hh + 1) * XA_DH]
            sc = _mm_nt(qc[:, ls], k_m) * (XA_DH ** -0.5)
            ex = jnp.exp(sc - jnp.max(sc, axis=-1, keepdims=True))
            att_s[:, ls] = _mm(ex, v_m) * (1.0 / jnp.sum(ex, axis=-1, keepdims=True))
            yield
        gated(2, _mm(att_s[...], wxa_ref[...]))

    _interleave(branch_a(), [gates(), branch_b(), branch_c()], FILL_RATIO)
    merged = y_s[0] + y_s[1] + y_s[2]
    out_ref[0] = x + _mm(merged, wo_ref[...])


def _level_masks(tm):
    r = np.arange(tm)[:, None]
    c = np.arange(tm)[None, :]
    same_chunk = (r // GDN_CHUNK) == (c // GDN_CHUNK)
    n_levels = GDN_CHUNK.bit_length() - 1
    masks = [same_chunk & (r > c) & (((r ^ c) >> j) == 1) for j in range(n_levels)]
    return jnp.asarray(np.stack(masks).astype(np.float32))


def _layer_spec(arr, layer):
    nd = arr.ndim - 1
    return pl.BlockSpec((None,) + arr.shape[1:], lambda b, s: (layer,) + (0,) * nd,
                        pipeline_mode=pl.Buffered(1))


def _const_spec(arr):
    nd = arr.ndim
    return pl.BlockSpec(arr.shape, lambda b, s: (0,) * nd, pipeline_mode=pl.Buffered(1))


def _w_in_splits(d_model):
    widths = (QKV_A, 2 * GDN_HEADS, V_A, 2 * CONV_CH, XA_W, N_BRANCH * d_model)
    edges = np.cumsum((0,) + widths)
    return tuple((int(c0), int(c1)) for c0, c1 in zip(edges[:-1], edges[1:]))


def _split_w_in_kernel(w_ref, *out_refs, splits):
    k_blk = w_ref.shape[2]
    eye = (lax.broadcasted_iota(jnp.int32, (k_blk, k_blk), 0)
           == lax.broadcasted_iota(jnp.int32, (k_blk, k_blk), 1)).astype(BF16)
    for (c0, c1), o_ref in zip(splits, out_refs):
        piece = w_ref[0, c0:c1, :]
        pad = o_ref.shape[2] - (c1 - c0)
        if pad:
            piece = jnp.concatenate([piece, jnp.zeros((pad, k_blk), piece.dtype)], axis=0)
        o_ref[0] = _mm_nt(eye, piece).astype(o_ref.dtype)


def _split_w_in(w_in):
    w_t = jnp.swapaxes(w_in, 1, 2)
    n_layers, n, k = w_t.shape
    splits = _w_in_splits(k)
    assert splits[-1][1] == n and k % SPLIT_COLS == 0
    assert all(c0 % SUBLANES == 0 for c0, _ in splits)
    widths = [-(-(c1 - c0) // LANES) * LANES for c0, c1 in splits]
    in_block = n * SPLIT_COLS * w_t.dtype.itemsize
    out_block = sum(widths) * SPLIT_COLS * jnp.dtype(BF16).itemsize
    return pl.pallas_call(
        functools.partial(_split_w_in_kernel, splits=splits),
        grid=(n_layers, k // SPLIT_COLS),
        in_specs=[pl.BlockSpec((1, n, SPLIT_COLS), lambda l, c: (l, 0, c))],
        out_specs=[pl.BlockSpec((1, SPLIT_COLS, wd), lambda l, c: (l, c, 0)) for wd in widths],
        out_shape=[jax.ShapeDtypeStruct((n_layers, k, wd), BF16) for wd in widths],
        compiler_params=pltpu.CompilerParams(
            dimension_semantics=("arbitrary", "arbitrary"),
            vmem_limit_bytes=2 * (in_block + out_block) + 3 * max(widths) * SPLIT_COLS * 4),
        name="split_w_in",
    )(w_t)


def _mixer_params(norm_mix, w_in, gdn_conv_w, gdn_dt_bias, gdn_a_log, gdn_norm, w_gdn_out,
                  cc_glu_b, cc_dw_w, cc_dw_b, cc_ln_w, cc_ln_b, w_cc_out, w_xa_out, gate_b, w_o):
    w_qkv, w_ab, w_z, w_glu, w_qc, w_gate = _split_w_in(w_in)
    pad_row = lambda v: jnp.pad(v, ((0, 0), (0, LANES - v.shape[1])))[:, None, :]
    row = lambda v: v[:, None, :]
    return [
        row(norm_mix), w_qkv, w_ab, w_z, w_glu, w_qc, w_gate,
        gdn_conv_w, pad_row(gdn_a_log), pad_row(gdn_dt_bias), row(gdn_norm), w_gdn_out.astype(BF16),
        row(cc_glu_b), cc_dw_w, row(cc_dw_b), row(cc_ln_w), row(cc_ln_b), w_cc_out.astype(BF16),
        w_xa_out.astype(BF16), row(gate_b), w_o.astype(BF16),
    ]


def _mixer(x, kvm, params, lvl, layer):
    bn, sn, d = x.shape
    tm = MIXER_ROWS
    mn, kvw = kvm.shape[2], kvm.shape[3]
    scratch = [
        ((tm, d), BF16),
        ((QKV_A // LANES, QKV_HALO + tm, LANES), F32),
        ((tm, QK_A), F32),
        ((tm, QK_A), F32),
        ((tm, V_A), F32),
        ((tm, V_A), F32),
        ((CONV_CH // LANES, CC_HALO + tm, LANES), F32),
        ((tm, CONV_CH), F32),
        ((tm, XA_W), F32),
        ((GDN_HEADS, GDN_DK, GDN_DV), F32),
        ((tm, N_BRANCH * d), F32),
        ((N_BRANCH, tm, d), F32),
    ]
    return pl.pallas_call(
        _mixer_kernel,
        grid=(bn, sn // tm),
        in_specs=[pl.BlockSpec((1, tm, d), lambda b, s: (b, s, 0)),
                  pl.BlockSpec((None, 1, mn, kvw), lambda b, s: (layer, b, 0, 0))]
                 + [_layer_spec(p, layer) for p in params] + [_const_spec(lvl)],
        out_specs=pl.BlockSpec((1, tm, d), lambda b, s: (b, s, 0)),
        out_shape=jax.ShapeDtypeStruct(x.shape, x.dtype),
        scratch_shapes=[pltpu.VMEM(shape, dtype) for shape, dtype in scratch],
        compiler_params=pltpu.CompilerParams(
            dimension_semantics=("arbitrary", "arbitrary"),
            vmem_limit_bytes=_vmem_limit(
                pipelined=[((tm, d), x.dtype), ((tm, d), x.dtype), ((mn, kvw), kvm.dtype)],
                resident=[(p.shape[1:], p.dtype) for p in params] + [(lvl.shape, lvl.dtype)],
                scratch=scratch,
                temp_bytes=2 * _nbytes((tm, QKV_A), F32) + 6 * GDN_HEADS * _nbytes((tm, tm), F32))),
        name="mixer",
    )(x, kvm, *params, lvl)


def _ffn_kernel(x_ref, nffn_ref, wup_ref, dww_ref, dwb_ref, wdown_ref, nfin_ref, out_ref,
                gbuf_s, act_s, *, final):
    tm = x_ref.shape[1]
    f = dww_ref.shape[1]

    @pl.when(pl.program_id(1) == 0)
    def _():
        gbuf_s[:, 0:FFN_HALO, :] = jnp.zeros((gbuf_s.shape[0], FFN_HALO, LANES), F32)

    x = x_ref[0]
    hb = _rmsnorm(x, nffn_ref[...]).astype(BF16)
    _to_slabs(gbuf_s, FFN_HALO, _mm(hb, wup_ref[:, :f]))
    act_s[...] = _mm(hb, wup_ref[:, f:])

    def emit(r0, c0, blk):
        rows, cols = slice(r0, r0 + CONV_ROWS), slice(c0, c0 + LANES)
        act_s[rows, cols] = _silu(blk + dwb_ref[:, cols]) * act_s[rows, cols]

    for _ in _causal_dwconv(gbuf_s, dww_ref, FFN_HALO, tm, FFN_CONV, emit):
        pass
    y = x + _mm(act_s[...], wdown_ref[...])
    if final:
        y = _rmsnorm(y, nfin_ref[...])
    out_ref[0] = y


def _ffn_params(norm_ffn, w_up, ffn_dw_w, ffn_dw_b, w_down):
    row = lambda v: v[:, None, :]
    return [row(norm_ffn), w_up.astype(BF16), ffn_dw_w, row(ffn_dw_b), w_down.astype(BF16)]


def _ffn(x, params, norm_final, layer, final):
    bn, sn, d = x.shape
    tm = FFN_ROWS
    f = params[4].shape[1]
    nfin = norm_final.reshape(1, d)
    scratch = [
        ((f // LANES, FFN_HALO + tm, LANES), F32),
        ((tm, f), F32),
    ]
    return pl.pallas_call(
        functools.partial(_ffn_kernel, final=final),
        grid=(bn, sn // tm),
        in_specs=[pl.BlockSpec((1, tm, d), lambda b, s: (b, s, 0))]
                 + [_layer_spec(p, layer) for p in params] + [_const_spec(nfin)],
        out_specs=pl.BlockSpec((1, tm, d), lambda b, s: (b, s, 0)),
        out_shape=jax.ShapeDtypeStruct(x.shape, x.dtype),
        scratch_shapes=[pltpu.VMEM(shape, dtype) for shape, dtype in scratch],
        compiler_params=pltpu.CompilerParams(
            dimension_semantics=("arbitrary", "arbitrary"),
            vmem_limit_bytes=_vmem_limit(
                pipelined=[((tm, d), x.dtype), ((tm, d), x.dtype)],
                resident=[(p.shape[1:], p.dtype) for p in params] + [(nfin.shape, nfin.dtype)],
                scratch=scratch,
                temp_bytes=_nbytes((tm, f), F32))),
        name="ffn",
    )(x, *params, nfin)


def kernel(x, mem, norm_mix, w_in, gdn_conv_w, gdn_dt_bias, gdn_a_log, gdn_norm, w_gdn_out, cc_glu_b, cc_dw_w, cc_dw_b, cc_ln_w, cc_ln_b, w_cc_out, mem_norm, w_mem_kv, w_xa_out, gate_b, w_o, norm_ffn, w_up, ffn_dw_w, ffn_dw_b, w_down, norm_final):
    n_layers = w_in.shape[0]
    assert x.shape[1] % MIXER_ROWS == 0 and x.shape[1] % FFN_ROWS == 0
    assert MIXER_ROWS % GDN_CHUNK == 0 and MIXER_ROWS % CONV_ROWS == 0 and FFN_ROWS % CONV_ROWS == 0
    kvm = _memkv(mem, mem_norm, w_mem_kv)
    mixer_params = _mixer_params(norm_mix, w_in, gdn_conv_w, gdn_dt_bias, gdn_a_log, gdn_norm,
                                 w_gdn_out, cc_glu_b, cc_dw_w, cc_dw_b, cc_ln_w, cc_ln_b, w_cc_out,
                                 w_xa_out, gate_b, w_o)
    ffn_params = _ffn_params(norm_ffn, w_up, ffn_dw_w, ffn_dw_b, w_down)
    lvl = _level_masks(MIXER_ROWS)
    for l in range(n_layers):
        x = _mixer(x, kvm, mixer_params, lvl, l)
        x = _ffn(x, ffn_params, norm_final, l, final=(l == n_layers - 1))
    return x
```

```python
import functools

import numpy as np
import jax
import jax.numpy as jnp
from jax import lax
from jax.experimental import pallas as pl
from jax.experimental.pallas import tpu as pltpu

EPS = 1e-6
GDN_HEADS = 4
GDN_DK = 128
GDN_DV = 128
GDN_CONV = 4
GDN_CHUNK = 64
QK_A = GDN_HEADS * GDN_DK
V_A = GDN_HEADS * GDN_DV
QKV_A = 2 * QK_A + V_A
CONV_CH = 512
CONV_K = 31
XA_HEADS = 4
XA_DH = 128
XA_W = XA_HEADS * XA_DH
N_BRANCH = 3
FFN_CONV = 3

LANES = 128
SUBLANES = 8
MIXER_ROWS = 256
FFN_ROWS = 512
SPLIT_COLS = 256
CONV_ROWS = 64
GATE_COLS = 256
FILL_RATIO = 2
HALF_ROWS_MIN = 4
QKV_HALO = SUBLANES
CC_HALO = 4 * SUBLANES
FFN_HALO = SUBLANES

BF16 = jnp.bfloat16
F32 = jnp.float32


def _nbytes(shape, dtype):
    return int(np.prod(shape)) * jnp.dtype(dtype).itemsize


def _vmem_limit(pipelined, resident, scratch, temp_bytes):
    return (2 * sum(_nbytes(*e) for e in pipelined) + sum(_nbytes(*e) for e in resident)
            + sum(_nbytes(*e) for e in scratch) + temp_bytes)


def _mm(a, b):
    return jnp.dot(a.astype(BF16), b.astype(BF16), preferred_element_type=F32)


def _mm_nt(a, b):
    return lax.dot_general(a.astype(BF16), b.astype(BF16), (((1,), (1,)), ((), ())),
                           preferred_element_type=F32)


def _mm_tn(a, b):
    return lax.dot_general(a.astype(BF16), b.astype(BF16), (((0,), (0,)), ((), ())),
                           preferred_element_type=F32)


def _rmsnorm(x, w_row):
    ms = jnp.mean(x * x, axis=-1, keepdims=True)
    return x * lax.rsqrt(ms + EPS) * w_row


def _sigmoid(x):
    return 0.5 * jnp.tanh(0.5 * x) + 0.5


def _silu(x):
    return x * _sigmoid(x)


def _softplus(x):
    return jnp.maximum(x, 0.0) + jnp.log(1.0 + jnp.exp(-jnp.abs(x)))


def _interleave(main, fillers, ratio):
    live = list(fillers)

    def fill(n):
        while n > 0 and live:
            for g in list(live):
                if n == 0:
                    break
                try:
                    next(g)
                    n -= 1
                except StopIteration:
                    live.remove(g)

    for _ in main:
        fill(ratio)
    fill(float("inf"))


def _split_rows(m, half):
    return m.reshape(m.shape[0] // (2 * half), 2 * half, m.shape[1])


def _upper_rows(m, half):
    return _split_rows(m, half)[:, :half, :].reshape(m.shape[0] // 2, m.shape[1])


def _lower_rows(m, half):
    return _split_rows(m, half)[:, half:, :].reshape(m.shape[0] // 2, m.shape[1])


def _merge_rows(upper, lower, half):
    n2, c = upper.shape
    return jnp.concatenate([upper.reshape(n2 // half, half, c), lower.reshape(n2 // half, half, c)],
                           axis=1).reshape(2 * n2, c)


def _to_slabs(buf_ref, halo, val):
    for cb in range(buf_ref.shape[0]):
        buf_ref[cb, halo:halo + val.shape[0], :] = val[:, cb * LANES:(cb + 1) * LANES]


def _causal_dwconv(buf_ref, w_ref, halo, n_rows, n_taps, emit):
    for cb in range(buf_ref.shape[0]):
        c0 = cb * LANES
        for r0 in range(0, n_rows, CONV_ROWS):
            acc = None
            for k in range(n_taps):
                start = halo - (n_taps - 1) + k + r0
                term = w_ref[k:k + 1, c0:c0 + LANES] * buf_ref[cb, start:start + CONV_ROWS, :]
                acc = term if acc is None else acc + term
            emit(r0, c0, acc)
            yield
    buf_ref[:, 0:halo, :] = buf_ref[:, n_rows:n_rows + halo, :]


def _memkv_kernel(mem_ref, nrm_ref, w_ref, out_ref):
    m = _rmsnorm(mem_ref[0], nrm_ref[0])
    out_ref[0, 0] = _mm(m, w_ref[0]).astype(out_ref.dtype)


def _memkv(mem, mem_norm, w_mem_kv):
    n_layers = w_mem_kv.shape[0]
    bn, mn, d = mem.shape
    n_out = w_mem_kv.shape[2]
    return pl.pallas_call(
        _memkv_kernel,
        grid=(n_layers, bn),
        in_specs=[
            pl.BlockSpec((1, mn, d), lambda l, b: (b, 0, 0)),
            pl.BlockSpec((1, 1, d), lambda l, b: (l, 0, 0)),
            pl.BlockSpec((1, d, n_out), lambda l, b: (l, 0, 0)),
        ],
        out_specs=pl.BlockSpec((1, 1, mn, n_out), lambda l, b: (l, b, 0, 0)),
        out_shape=jax.ShapeDtypeStruct((n_layers, bn, mn, n_out), BF16),
        compiler_params=pltpu.CompilerParams(
            dimension_semantics=("arbitrary", "arbitrary"),
            vmem_limit_bytes=_vmem_limit(
                pipelined=[((mn, d), mem.dtype), ((SUBLANES, d), F32), ((d, n_out), BF16), ((mn, n_out), BF16)],
                resident=[], scratch=[],
                temp_bytes=_nbytes((mn, d), F32) + _nbytes((mn, n_out), F32))),
        name="mem_kv",
    )(mem, mem_norm.reshape(n_layers, 1, d), w_mem_kv.astype(BF16))


def _mixer_kernel(x_ref, kvm_ref, nmix_ref, wqkv_ref, wab_ref, wz_ref, wglu_ref, wqc_ref,
                  wgate_ref, convw_ref, alog_ref, dtb_ref, gnorm_ref, wgdn_ref,
                  glub_ref, dww_ref, dwb_ref, lnw_ref, lnb_ref, wcc_ref,
                  wxa_ref, gateb_ref, wo_ref, lvl_ref,
                  out_ref,
                  h_s, qkv_s, q_s, k_s, v_s, o_s, ubuf_s, conv_s, att_s, state_s, gate_s, y_s):
    tm = x_ref.shape[1]
    d = x_ref.shape[2]

    @pl.when(pl.program_id(1) == 0)
    def _():
        qkv_s[:, 0:QKV_HALO, :] = jnp.zeros((qkv_s.shape[0], QKV_HALO, LANES), F32)
        ubuf_s[:, 0:CC_HALO, :] = jnp.zeros((ubuf_s.shape[0], CC_HALO, LANES), F32)
        state_s[...] = jnp.zeros(state_s.shape, F32)

    x = x_ref[0]
    h_s[...] = _rmsnorm(x, nmix_ref[...]).astype(BF16)

    gate_cols_written = set()

    def gates():
        for c0 in range(0, N_BRANCH * d, GATE_COLS):
            cols = slice(c0, c0 + GATE_COLS)
            gate_s[:, cols] = _sigmoid(_mm(h_s[...], wgate_ref[:, cols]) + gateb_ref[:, cols])
            gate_cols_written.add(c0)
            yield

    def gated(branch, y):
        assert all(c0 in gate_cols_written for c0 in range(branch * d, (branch + 1) * d, GATE_COLS))
        y_s[branch] = gate_s[:, branch * d:(branch + 1) * d] * y

    def branch_a():
        ab = _mm(h_s[...], wab_ref[...])
        g = -jnp.exp(alog_ref[...]) * _softplus(ab + dtb_ref[...])
        beta = _sigmoid(ab)

        n_chunks = tm // GDN_CHUNK
        chunk_bits = GDN_CHUNK.bit_length() - 1
        ri = lax.broadcasted_iota(jnp.int32, (tm, tm), 0)
        ci = lax.broadcasted_iota(jnp.int32, (tm, tm), 1)
        same_chunk = (ri >> chunk_bits) == (ci >> chunk_bits)
        causal = same_chunk & (ri >= ci)
        strict = same_chunk & (ri > ci)
        eye = (ri == ci).astype(F32)

        tri = causal.astype(BF16)
        g_hi = g.astype(BF16)
        g_r1 = g - g_hi.astype(F32)
        g_mid = g_r1.astype(BF16)
        g_lo = (g_r1 - g_mid.astype(F32)).astype(BF16)
        gc = _mm(tri, g_hi) + _mm(tri, g_mid) + _mm(tri, g_lo)
        gct = gc.T
        eg = jnp.exp(gc)
        egl_parts, eg_last = [], []
        for c in range(n_chunks):
            gl = gc[(c + 1) * GDN_CHUNK - 1:(c + 1) * GDN_CHUNK, :]
            egl_parts.append(jnp.exp(gl - gc[c * GDN_CHUNK:(c + 1) * GDN_CHUNK, :]))
            eg_last.append(jnp.exp(gl))
        egl = jnp.concatenate(egl_parts, axis=0)
        heads = range(GDN_HEADS)
        decay = [jnp.exp(jnp.where(causal, gc[:, hh:hh + 1] - gct[hh:hh + 1, :], -jnp.inf))
                 for hh in heads]
        yield

        _to_slabs(qkv_s, QKV_HALO, _mm(h_s[...], wqkv_ref[...]))
        yield

        def emit_qkv(r0, c0, blk):
            blk = _silu(blk)
            rows = slice(r0, r0 + CONV_ROWS)
            if c0 < 2 * QK_A:
                blk = blk * lax.rsqrt(jnp.sum(blk * blk, axis=-1, keepdims=True) + EPS)
            if c0 < QK_A:
                q_s[rows, c0:c0 + LANES] = blk * (GDN_DK ** -0.5)
            elif c0 < 2 * QK_A:
                k_s[rows, c0 - QK_A:c0 - QK_A + LANES] = blk
            else:
                v_s[rows, c0 - 2 * QK_A:c0 - 2 * QK_A + LANES] = blk

        for _ in _causal_dwconv(qkv_s, convw_ref, QKV_HALO, tm, GDN_CONV, emit_qkv):
            pass
        yield

        lanes = [slice(hh * GDN_DK, (hh + 1) * GDN_DK) for hh in heads]
        q_h = [q_s[:, ls] for ls in lanes]
        k_h = [k_s[:, ls] for ls in lanes]
        bcol = [beta[:, GDN_HEADS + hh:GDN_HEADS + hh + 1] for hh in heads]
        egcol = [eg[:, hh:hh + 1] for hh in heads]
        kb = [k_h[hh] * bcol[hh] for hh in heads]
        a_mat = [jnp.where(strict, _mm_nt(kb[hh], k_h[hh]) * decay[hh], 0.0) for hh in heads]
        yield
        qk = [jnp.where(causal, _mm_nt(q_h[hh], k_h[hh]) * decay[hh], 0.0) for hh in heads]
        t_mat = [eye - a_mat[hh] * lvl_ref[0] for hh in heads]
        yield
        for j in range(1, chunk_bits):
            half = 1 << j
            if half < HALF_ROWS_MIN:
                xt = [_mm(a_mat[hh] * lvl_ref[j], t_mat[hh]) for hh in heads]
                yield
                t_mat = [t_mat[hh] - _mm(t_mat[hh], xt[hh]) for hh in heads]
                yield
                continue
            sel = _lower_rows(lvl_ref[j], half)
            xl = [_mm(_lower_rows(a_mat[hh], half) * sel, t_mat[hh]) for hh in heads]
            yield
            zeros = jnp.zeros((tm // 2, tm), F32)
            upd = [_mm(_lower_rows(t_mat[hh], half), _merge_rows(zeros, xl[hh], half)) for hh in heads]
            t_mat = [_merge_rows(_upper_rows(t_mat[hh], half), _lower_rows(t_mat[hh], half) - upd[hh], half)
                     for hh in heads]
            yield
        uw = [_mm(t_mat[hh], jnp.concatenate([v_s[:, lanes[hh]] * bcol[hh], kb[hh] * egcol[hh]], axis=1))
              for hh in heads]
        yield
        loc = [_mm(qk[hh], uw[hh]) for hh in heads]
        k_dec = [k_h[hh] * egl[:, hh:hh + 1] for hh in heads]
        chunk_rows = [slice(c * GDN_CHUNK, (c + 1) * GDN_CHUNK) for c in range(n_chunks)]
        kuw = [[_mm_tn(k_dec[hh][rows], uw[hh][rows]) for rows in chunk_rows] for hh in heads]
        yield
        q_til = [q_h[hh] * egcol[hh] - loc[hh][:, GDN_DV:] for hh in heads]
        st = [state_s[hh] for hh in heads]
        o_parts = [[] for _ in heads]
        for c, rows in enumerate(chunk_rows):
            for hh in heads:
                r = _mm(jnp.concatenate([kuw[hh][c][:, GDN_DV:], q_til[hh][rows]], axis=0), st[hh])
                o_parts[hh].append(r[GDN_DK:])
                st[hh] = st[hh] * eg_last[c][:, hh:hh + 1] + kuw[hh][c][:, :GDN_DV] - r[:GDN_DK]
            yield
        z = _mm(h_s[...], wz_ref[...])
        for hh in heads:
            state_s[hh] = st[hh]
            o = jnp.concatenate(o_parts[hh], axis=0) + loc[hh][:, :GDN_DV]
            o_s[:, lanes[hh]] = _rmsnorm(o, gnorm_ref[...]) * _silu(z[:, lanes[hh]])
        yield
        gated(0, _mm(o_s[...], wgdn_ref[...]))

    def branch_b():
        glu = _mm(h_s[...], wglu_ref[...]) + glub_ref[...]
        _to_slabs(ubuf_s, CC_HALO, glu[:, :CONV_CH] * _sigmoid(glu[:, CONV_CH:]))
        yield

        def emit_cc(r0, c0, blk):
            conv_s[r0:r0 + CONV_ROWS, c0:c0 + LANES] = blk + dwb_ref[:, c0:c0 + LANES]

        yield from _causal_dwconv(ubuf_s, dww_ref, CC_HALO, tm, CONV_K, emit_cc)
        for r0 in range(0, tm, CONV_ROWS):
            blk = conv_s[r0:r0 + CONV_ROWS, :]
            mu = jnp.mean(blk, axis=-1, keepdims=True)
            cen = blk - mu
            var = jnp.mean(cen * cen, axis=-1, keepdims=True)
            conv_s[r0:r0 + CONV_ROWS, :] = _silu(cen * lax.rsqrt(var + EPS) * lnw_ref[...] + lnb_ref[...])
            yield
        gated(1, _mm(conv_s[...], wcc_ref[...]))

    def branch_c():
        qc = _mm(h_s[...], wqc_ref[...])
        yield
        for hh in range(XA_HEADS):
            ls = slice(hh * XA_DH, (hh + 1) * XA_DH)
            k_m = kvm_ref[0, :, hh * XA_DH:(hh + 1) * XA_DH]
            v_m = kvm_ref[0, :, XA_W + hh * XA_DH:XA_W + (hh + 1) * XA_DH]
            sc = _mm_nt(qc[:, ls], k_m) * (XA_DH ** -0.5)
            ex = jnp.exp(sc - jnp.max(sc, axis=-1, keepdims=True))
            att_s[:, ls] = _mm(ex, v_m) * (1.0 / jnp.sum(ex, axis=-1, keepdims=True))
            yield
        gated(2, _mm(att_s[...], wxa_ref[...]))

    _interleave(branch_a(), [gates(), branch_b(), branch_c()], FILL_RATIO)
    merged = y_s[0] + y_s[1] + y_s[2]
    out_ref[0] = x + _mm(merged, wo_ref[...])


def _level_masks(tm):
    r = np.arange(tm)[:, None]
    c = np.arange(tm)[None, :]
    same_chunk = (r // GDN_CHUNK) == (c // GDN_CHUNK)
    n_levels = GDN_CHUNK.bit_length() - 1
    masks = [same_chunk & (r > c) & (((r ^ c) >> j) == 1) for j in range(n_levels)]
    return jnp.asarray(np.stack(masks).astype(np.float32))


def _layer_spec(arr, layer):
    nd = arr.ndim - 1
    return pl.BlockSpec((None,) + arr.shape[1:], lambda b, s: (layer,) + (0,) * nd,
                        pipeline_mode=pl.Buffered(1))


def _const_spec(arr):
    nd = arr.ndim
    return pl.BlockSpec(arr.shape, lambda b, s: (0,) * nd, pipeline_mode=pl.Buffered(1))


def _w_in_splits(d_model):
    widths = (QKV_A, 2 * GDN_HEADS, V_A, 2 * CONV_CH, XA_W, N_BRANCH * d_model)
    edges = np.cumsum((0,) + widths)
    return tuple((int(c0), int(c1)) for c0, c1 in zip(edges[:-1], edges[1:]))


def _split_w_in_kernel(w_ref, *out_refs, splits):
    k_blk = w_ref.shape[2]
    eye = (lax.broadcasted_iota(jnp.int32, (k_blk, k_blk), 0)
           == lax.broadcasted_iota(jnp.int32, (k_blk, k_blk), 1)).astype(BF16)
    for (c0, c1), o_ref in zip(splits, out_refs):
        piece = w_ref[0, c0:c1, :]
        pad = o_ref.shape[2] - (c1 - c0)
        if pad:
            piece = jnp.concatenate([piece, jnp.zeros((pad, k_blk), piece.dtype)], axis=0)
        o_ref[0] = _mm_nt(eye, piece).astype(o_ref.dtype)


def _split_w_in(w_in):
    w_t = jnp.swapaxes(w_in, 1, 2)
    n_layers, n, k = w_t.shape
    splits = _w_in_splits(k)
    assert splits[-1][1] == n and k % SPLIT_COLS == 0
    assert all(c0 % SUBLANES == 0 for c0, _ in splits)
    widths = [-(-(c1 - c0) // LANES) * LANES for c0, c1 in splits]
    in_block = n * SPLIT_COLS * w_t.dtype.itemsize
    out_block = sum(widths) * SPLIT_COLS * jnp.dtype(BF16).itemsize
    return pl.pallas_call(
        functools.partial(_split_w_in_kernel, splits=splits),
        grid=(n_layers, k // SPLIT_COLS),
        in_specs=[pl.BlockSpec((1, n, SPLIT_COLS), lambda l, c: (l, 0, c))],
        out_specs=[pl.BlockSpec((1, SPLIT_COLS, wd), lambda l, c: (l, c, 0)) for wd in widths],
        out_shape=[jax.ShapeDtypeStruct((n_layers, k, wd), BF16) for wd in widths],
        compiler_params=pltpu.CompilerParams(
            dimension_semantics=("arbitrary", "arbitrary"),
            vmem_limit_bytes=2 * (in_block + out_block) + 3 * max(widths) * SPLIT_COLS * 4),
        name="split_w_in",
    )(w_t)


def _mixer_params(norm_mix, w_in, gdn_conv_w, gdn_dt_bias, gdn_a_log, gdn_norm, w_gdn_out,
                  cc_glu_b, cc_dw_w, cc_dw_b, cc_ln_w, cc_ln_b, w_cc_out, w_xa_out, gate_b, w_o):
    w_qkv, w_ab, w_z, w_glu, w_qc, w_gate = _split_w_in(w_in)
    pad_row = lambda v: jnp.pad(v, ((0, 0), (0, LANES - v.shape[1])))[:, None, :]
    row = lambda v: v[:, None, :]
    return [
        row(norm_mix), w_qkv, w_ab, w_z, w_glu, w_qc, w_gate,
        gdn_conv_w, pad_row(gdn_a_log), pad_row(gdn_dt_bias), row(gdn_norm), w_gdn_out.astype(BF16),
        row(cc_glu_b), cc_dw_w, row(cc_dw_b), row(cc_ln_w), row(cc_ln_b), w_cc_out.astype(BF16),
        w_xa_out.astype(BF16), row(gate_b), w_o.astype(BF16),
    ]


def _mixer(x, kvm, params, lvl, layer):
    bn, sn, d = x.shape
    tm = MIXER_ROWS
    mn, kvw = kvm.shape[2], kvm.shape[3]
    scratch = [
        ((tm, d), BF16),
        ((QKV_A // LANES, QKV_HALO + tm, LANES), F32),
        ((tm, QK_A), F32),
        ((tm, QK_A), F32),
        ((tm, V_A), F32),
        ((tm, V_A), F32),
        ((CONV_CH // LANES, CC_HALO + tm, LANES), F32),
        ((tm, CONV_CH), F32),
        ((tm, XA_W), F32),
        ((GDN_HEADS, GDN_DK, GDN_DV), F32),
        ((tm, N_BRANCH * d), F32),
        ((N_BRANCH, tm, d), F32),
    ]
    return pl.pallas_call(
        _mixer_kernel,
        grid=(bn, sn // tm),
        in_specs=[pl.BlockSpec((1, tm, d), lambda b, s: (b, s, 0)),
                  pl.BlockSpec((None, 1, mn, kvw), lambda b, s: (layer, b, 0, 0))]
                 + [_layer_spec(p, layer) for p in params] + [_const_spec(lvl)],
        out_specs=pl.BlockSpec((1, tm, d), lambda b, s: (b, s, 0)),
        out_shape=jax.ShapeDtypeStruct(x.shape, x.dtype),
        scratch_shapes=[pltpu.VMEM(shape, dtype) for shape, dtype in scratch],
        compiler_params=pltpu.CompilerParams(
            dimension_semantics=("arbitrary", "arbitrary"),
            vmem_limit_bytes=_vmem_limit(
                pipelined=[((tm, d), x.dtype), ((tm, d), x.dtype), ((mn, kvw), kvm.dtype)],
                resident=[(p.shape[1:], p.dtype) for p in params] + [(lvl.shape, lvl.dtype)],
                scratch=scratch,
                temp_bytes=2 * _nbytes((tm, QKV_A), F32) + 6 * GDN_HEADS * _nbytes((tm, tm), F32))),
        name="mixer",
    )(x, kvm, *params, lvl)


def _ffn_kernel(x_ref, nffn_ref, wup_ref, dww_ref, dwb_ref, wdown_ref, nfin_ref, out_ref,
                gbuf_s, act_s, *, final):
    tm = x_ref.shape[1]
    f = dww_ref.shape[1]

    @pl.when(pl.program_id(1) == 0)
    def _():
        gbuf_s[:, 0:FFN_HALO, :] = jnp.zeros((gbuf_s.shape[0], FFN_HALO, LANES), F32)

    x = x_ref[0]
    hb = _rmsnorm(x, nffn_ref[...]).astype(BF16)
    _to_slabs(gbuf_s, FFN_HALO, _mm(hb, wup_ref[:, :f]))
    act_s[...] = _mm(hb, wup_ref[:, f:])

    def emit(r0, c0, blk):
        rows, cols = slice(r0, r0 + CONV_ROWS), slice(c0, c0 + LANES)
        act_s[rows, cols] = _silu(blk + dwb_ref[:, cols]) * act_s[rows, cols]

    for _ in _causal_dwconv(gbuf_s, dww_ref, FFN_HALO, tm, FFN_CONV, emit):
        pass
    y = x + _mm(act_s[...], wdown_ref[...])
    if final:
        y = _rmsnorm(y, nfin_ref[...])
    out_ref[0] = y


def _ffn_params(norm_ffn, w_up, ffn_dw_w, ffn_dw_b, w_down):
    row = lambda v: v[:, None, :]
    return [row(norm_ffn), w_up.astype(BF16), ffn_dw_w, row(ffn_dw_b), w_down.astype(BF16)]


def _ffn(x, params, norm_final, layer, final):
    bn, sn, d = x.shape
    tm = FFN_ROWS
    f = params[4].shape[1]
    nfin = norm_final.reshape(1, d)
    scratch = [
        ((f // LANES, FFN_HALO + tm, LANES), F32),
        ((tm, f), F32),
    ]
    return pl.pallas_call(
        functools.partial(_ffn_kernel, final=final),
        grid=(bn, sn // tm),
        in_specs=[pl.BlockSpec((1, tm, d), lambda b, s: (b, s, 0))]
                 + [_layer_spec(p, layer) for p in params] + [_const_spec(nfin)],
        out_specs=pl.BlockSpec((1, tm, d), lambda b, s: (b, s, 0)),
        out_shape=jax.ShapeDtypeStruct(x.shape, x.dtype),
        scratch_shapes=[pltpu.VMEM(shape, dtype) for shape, dtype in scratch],
        compiler_params=pltpu.CompilerParams(
            dimension_semantics=("arbitrary", "arbitrary"),
            vmem_limit_bytes=_vmem_limit(
                pipelined=[((tm, d), x.dtype), ((tm, d), x.dtype)],
                resident=[(p.shape[1:], p.dtype) for p in params] + [(nfin.shape, nfin.dtype)],
                scratch=scratch,
                temp_bytes=_nbytes((tm, f), F32))),
        name="ffn",
    )(x, *params, nfin)


def kernel(x, mem, norm_mix, w_in, gdn_conv_w, gdn_dt_bias, gdn_a_log, gdn_norm, w_gdn_out, cc_glu_b, cc_dw_w, cc_dw_b, cc_ln_w, cc_ln_b, w_cc_out, mem_norm, w_mem_kv, w_xa_out, gate_b, w_o, norm_ffn, w_up, ffn_dw_w, ffn_dw_b, w_down, norm_final):
    n_layers = w_in.shape[0]
    assert x.shape[1] % MIXER_ROWS == 0 and x.shape[1] % FFN_ROWS == 0
    assert MIXER_ROWS % GDN_CHUNK == 0 and MIXER_ROWS % CONV_ROWS == 0 and FFN_ROWS % CONV_ROWS == 0
    kvm = _memkv(mem, mem_norm, w_mem_kv)
    mixer_params = _mixer_params(norm_mix, w_in, gdn_conv_w, gdn_dt_bias, gdn_a_log, gdn_norm,
                                 w_gdn_out, cc_glu_b, cc_dw_w, cc_dw_b, cc_ln_w, cc_ln_b, w_cc_out,
                                 w_xa_out, gate_b, w_o)
    ffn_params = _ffn_params(norm_ffn, w_up, ffn_dw_w, ffn_dw_b, w_down)
    lvl = _level_masks(MIXER_ROWS)
    for l in range(n_layers):
        x = _mixer(x, kvm, mixer_params, lvl, l)
        x = _ffn(x, ffn_params, norm_final, l, final=(l == n_layers - 1))
    return x
```
